```python
import math
import jax
import jax.numpy as jnp
from jax import lax
import numpy as np

D_MODEL = 2048
BATCH = 32
SEQ = 256
DEPTH = 4
DEC_BATCH = 2
DEC_SEQ = 1024
PAST_LEN = 256

GRID_W = 64
ROPE_THETA = 10000.0
Q_BLOCK = 128
EPS = 1e-6
HEAD_DIM = 128
N_EVEN = (DEPTH + 1) // 2
N_ODD = DEPTH // 2

DA_HEADS = 8
DA_SUB = HEAD_DIM // 2
GQ_HEADS = 8
GQ_KV_HEADS = 2
GQ_REP = GQ_HEADS // GQ_KV_HEADS
ML_HEADS = 8
ML_DK = 128
ML_DV = 128
ML_CHUNK = 64
ML_CONV = 3
MLA_HEADS = 8
MLA_Q_RANK = 512
MLA_KV_RANK = 256
MLA_NOPE = 128
MLA_ROPE = 64
MLA_V = 128
D_FF = 5632
FFN_CONV = 3

EVEN_WIDTHS = (DA_HEADS * HEAD_DIM, DA_HEADS * HEAD_DIM, DA_HEADS * HEAD_DIM,
               GQ_HEADS * HEAD_DIM, GQ_KV_HEADS * HEAD_DIM, GQ_KV_HEADS * HEAD_DIM)
ODD_WIDTHS = (ML_HEADS * ML_DK, ML_HEADS * ML_DK, ML_HEADS * ML_DV, ML_HEADS * ML_DV,
              4 * ML_HEADS, MLA_Q_RANK, MLA_KV_RANK, MLA_ROPE)
EVEN_IN = sum(EVEN_WIDTHS)
ODD_IN = sum(ODD_WIDTHS)
EVEN_MIX = DA_HEADS * HEAD_DIM + GQ_HEADS * HEAD_DIM
ODD_MIX = ML_HEADS * ML_DV + MLA_HEADS * MLA_V

kernel_name = 'hybrid_diffusion_prefix_trunk_step'


def _split(x, widths):
    offs, acc = [], 0
    for w in widths[:-1]:
        acc += w
        offs.append(acc)
    return jnp.split(x, offs, axis=-1)


def _rms(x, g):
    xf = x.astype(jnp.float32)
    y = xf * lax.rsqrt(jnp.mean(xf * xf, axis=-1, keepdims=True) + EPS)
    return (y * g.astype(jnp.float32)).astype(x.dtype)


def _adaln(cvec, w, b):
    m = (jax.nn.silu(cvec) @ w + b)[:, None, :]
    return jnp.split(m, 6, axis=-1)


def _modulate(x, g, shift, scale):
    return _rms(x, g) * (1 + scale) + shift


def _dwconv(x, w, b):
    width = w.shape[0]
    pad = width // 2
    n = x.shape[1]
    xp = jnp.pad(x, ((0, 0), (pad, pad), (0, 0)))
    y = b
    for i in range(width):
        y = y + xp[:, i:i + n] * w[i]
    return y


def _rope_tables(grid_rows, dim):
    t = jnp.arange(grid_rows * GRID_W)
    r = (t // GRID_W).astype(jnp.float32)
    col = (t % GRID_W).astype(jnp.float32)
    quarter = dim // 4
    inv = ROPE_THETA ** (-jnp.arange(quarter, dtype=jnp.float32) / quarter)
    ang = jnp.stack([r[:, None] * inv, col[:, None] * inv], axis=1)
    return jnp.cos(ang), jnp.sin(ang)


def _rope(x, cos, sin):
    quarter = x.shape[-1] // 4
    xr = x.astype(jnp.float32).reshape(x.shape[:-1] + (2, 2, quarter))
    x1, x2 = xr[..., 0, :], xr[..., 1, :]
    out = jnp.stack([x1 * cos - x2 * sin, x1 * sin + x2 * cos], axis=-2)
    return out.reshape(x.shape).astype(x.dtype)


def _over_query_blocks(fn, queries):
    n_q = queries[0].shape[-2]
    nb = n_q // Q_BLOCK

    def split(a):
        a = a.reshape(a.shape[:-2] + (nb, Q_BLOCK, a.shape[-1]))
        return jnp.moveaxis(a, -3, 0)

    out = lax.map(lambda blk: fn(*blk), tuple(split(a) for a in queries))
    out = jnp.moveaxis(out, 0, -3)
    return out.reshape(out.shape[:-3] + (n_q, out.shape[-1]))


def _diff_attention(q, k, v, lam, ln_g, lam_init):
    scale = DA_SUB ** -0.5

    def block(qb):
        s = jnp.einsum('bhcqd,bhckd->bhcqk', qb, k, preferred_element_type=jnp.float32) * scale
        a = jax.nn.softmax(s, axis=-1)
        w = a[:, :, 0] - lam * a[:, :, 1]
        return jnp.einsum('bhqk,bhkd->bhqd', w.astype(v.dtype), v)

    o = _over_query_blocks(block, (q,))
    return _rms(o, ln_g) * (1.0 - lam_init)


def _gqa(q, k, v):
    scale = HEAD_DIM ** -0.5

    def block(qb):
        s = jnp.einsum('bgrqd,bgkd->bgrqk', qb, k, preferred_element_type=jnp.float32) * scale
        a = jax.nn.softmax(s, axis=-1)
        return jnp.einsum('bgrqk,bgkd->bgrqd', a.astype(v.dtype), v)

    return _over_query_blocks(block, (q,))


def _mla_attention(q_nope, q_pe, k_nope, k_pe, v):
    scale = (MLA_NOPE + MLA_ROPE) ** -0.5

    def block(qn, qp):
        s = (jnp.einsum('bhqd,bhkd->bhqk', qn, k_nope, preferred_element_type=jnp.float32)
             + jnp.einsum('bhqd,bkd->bhqk', qp, k_pe, preferred_element_type=jnp.float32)) * scale
        a = jax.nn.softmax(s, axis=-1)
        return jnp.einsum('bhqk,bhkd->bhqd', a.astype(v.dtype), v)

    return _over_query_blocks(block, (q_nope, q_pe))


def _mla_kv(ckv, w_ukv):
    b_, s_, _ = ckv.shape
    kv = (ckv @ w_ukv).reshape(b_, s_, MLA_HEADS, MLA_NOPE + MLA_V).transpose(0, 2, 1, 3)
    return kv[..., :MLA_NOPE], kv[..., MLA_NOPE:]


def _mlstm_scan(q, k, v, log_i, log_f, c0, n0, m0):
    b_, h_, n_tok, _ = q.shape
    nc = n_tok // ML_CHUNK
    f32 = jnp.float32

    def chunks(a):
        a = a.astype(f32)
        a = a.reshape(a.shape[:2] + (nc, ML_CHUNK) + a.shape[3:])
        return jnp.moveaxis(a, 2, 0)

    tril = jnp.tril(jnp.ones((ML_CHUNK, ML_CHUNK), dtype=bool))

    def step(carry, inp):
        cst, nst, mst = carry
        qc, kc, vc, li, lf = inp
        bcum = jnp.cumsum(lf, axis=-1)
        d_intra = jnp.where(tril, bcum[..., :, None] - bcum[..., None, :] + li[..., None, :], -jnp.inf)
        d_inter = bcum + mst[..., None]
        m_t = jnp.maximum(d_inter, jnp.max(d_intra, axis=-1))
        s = jnp.einsum('bhtd,bhsd->bhts', qc, kc) * jnp.exp(d_intra - m_t[..., None])
        a = jnp.exp(d_inter - m_t)
        num = jnp.einsum('bhts,bhsv->bhtv', s, vc) + a[..., None] * jnp.einsum('bhtd,bhdv->bhtv', qc, cst)
        den = jnp.sum(s, axis=-1) + a * jnp.einsum('bhtd,bhd->bht', qc, nst)
        h = num / jnp.maximum(jnp.abs(den), jnp.exp(-m_t))[..., None]
        b_end = bcum[..., -1]
        g = b_end[..., None] - bcum + li
        m_new = jnp.maximum(b_end + mst, jnp.max(g, axis=-1))
        wg = jnp.exp(g - m_new[..., None])
        decay = jnp.exp(b_end + mst - m_new)
        c_new = decay[..., None, None] * cst + jnp.einsum('bhs,bhsd,bhsv->bhdv', wg, kc, vc)
        n_new = decay[..., None] * nst + jnp.einsum('bhs,bhsd->bhd', wg, kc)
        return (c_new, n_new, m_new), h

    init = (c0.astype(f32), n0.astype(f32), m0.astype(f32))
    (c_fin, n_fin, m_fin), hs = lax.scan(step, init, (chunks(q), chunks(k), chunks(v), chunks(log_i), chunks(log_f)))
    h = jnp.moveaxis(hs, 0, 2).reshape(b_, h_, n_tok, hs.shape[-1])
    return h, c_fin, n_fin, m_fin


def _mlstm_mixer(qm, km, vm, om, gates, gate_b, norm_g, c0, n0, m0):
    b_, h_, n_tok, _ = qm.shape
    g = (gates + gate_b).astype(jnp.float32).reshape(b_, n_tok, 4, ML_HEADS).transpose(2, 0, 3, 1)
    i_f, lf_f = g[0], jax.nn.log_sigmoid(g[1])
    i_b, lf_b = g[2], jax.nn.log_sigmoid(g[3])
    h_f, c_f, n_f, m_f = _mlstm_scan(qm, km, vm, i_f, lf_f, c0[:, 0], n0[:, 0], m0[:, 0])

    def rev(a):
        return jnp.flip(a, axis=2)

    h_b, c_b, n_b, m_b = _mlstm_scan(rev(qm), rev(km), rev(vm), rev(i_b), rev(lf_b), c0[:, 1], n0[:, 1], m0[:, 1])
    h = _rms(h_f + rev(h_b), norm_g)
    h = h.transpose(0, 2, 1, 3).reshape(b_, n_tok, h_ * ML_DV).astype(om.dtype) * jax.nn.sigmoid(om)
    return h, (jnp.stack([c_f, c_b], axis=1), jnp.stack([n_f, n_b], axis=1), jnp.stack([m_f, m_b], axis=1))


def _even_heads(h, w_in, q_norm_g, k_norm_g):
    b_, n_tok, _ = h.shape
    qa, ka, va, qb, kb, vb = _split(h @ w_in, EVEN_WIDTHS)

    def sub(a):
        return a.reshape(b_, n_tok, DA_HEADS, 2, DA_SUB).transpose(0, 2, 3, 1, 4)

    qa, ka = sub(qa), sub(ka)
    va = va.reshape(b_, n_tok, DA_HEADS, HEAD_DIM).transpose(0, 2, 1, 3)
    qb = _rms(qb.reshape(b_, n_tok, GQ_KV_HEADS, GQ_REP, HEAD_DIM), q_norm_g).transpose(0, 2, 3, 1, 4)
    kb = _rms(kb.reshape(b_, n_tok, GQ_KV_HEADS, HEAD_DIM), k_norm_g).transpose(0, 2, 1, 3)
    vb = vb.reshape(b_, n_tok, GQ_KV_HEADS, HEAD_DIM).transpose(0, 2, 1, 3)
    return qa, ka, va, qb, kb, vb


def _even_merge(oa, ob, w_out):
    b_, _, n_tok, _ = oa.shape
    oa = oa.transpose(0, 2, 1, 3).reshape(b_, n_tok, DA_HEADS * HEAD_DIM)
    ob = ob.transpose(0, 3, 1, 2, 4).reshape(b_, n_tok, GQ_HEADS * HEAD_DIM)
    return jnp.concatenate([oa, ob], axis=-1) @ w_out


def _odd_heads(h, w_in, conv_w, conv_b, q_norm_g, w_uq, kv_norm_g):
    b_, n_tok, _ = h.shape
    qm, km, vm, om, gates, q_down, kv_down, k_pe = _split(h @ w_in, ODD_WIDTHS)
    qk = jax.nn.silu(_dwconv(jnp.concatenate([qm, km], axis=-1), conv_w, conv_b))
    qm, km = jnp.split(qk, 2, axis=-1)

    def heads(a, d):
        return a.reshape(b_, n_tok, ML_HEADS, d).transpose(0, 2, 1, 3)

    qm = heads(qm, ML_DK) * (ML_DK ** -0.5)
    km = heads(km, ML_DK)
    vm = heads(vm, ML_DV)
    cq = (_rms(q_down, q_norm_g) @ w_uq).reshape(b_, n_tok, MLA_HEADS, MLA_NOPE + MLA_ROPE).transpose(0, 2, 1, 3)
    ckv = _rms(kv_down, kv_norm_g)
    return qm, km, vm, om, gates, cq[..., :MLA_NOPE], cq[..., MLA_NOPE:], ckv, k_pe


def _odd_merge(h_ml, h_mla, w_out):
    b_, _, n_tok, _ = h_mla.shape
    h_mla = h_mla.transpose(0, 2, 1, 3).reshape(b_, n_tok, MLA_HEADS * MLA_V)
    return jnp.concatenate([h_ml, h_mla], axis=-1) @ w_out


def _conv_ffn(h, w_up, conv_w, conv_b, w_down):
    u = _dwconv(h @ w_up, conv_w, conv_b)
    g, up = jnp.split(u, 2, axis=-1)
    return (jax.nn.silu(g) * up) @ w_down


def setup_inputs(seed: int = 0) -> dict:
    key = jax.random.key(seed)
    keys = iter(jax.random.split(key, 64))

    def nrm(shape, scale=1.0):
        return scale * jax.random.normal(next(keys), shape, dtype=jnp.float32)

    def gain(shape):
        return 1.0 + 0.05 * nrm(shape)

    D = D_MODEL
    f_bias = jnp.linspace(3.0, 6.0, ML_HEADS, dtype=jnp.float32)
    i_bias = jnp.zeros((ML_HEADS,), jnp.float32)
    gate_base = jnp.concatenate([i_bias, f_bias, i_bias, f_bias])
    return dict(
        x_prompt=nrm((BATCH, SEQ, D)),
        x_sample=nrm((DEC_BATCH, DEC_SEQ, D)),
        cache_da_k=nrm((DEC_BATCH, N_EVEN, DA_HEADS, 2, PAST_LEN, DA_SUB)),
        cache_da_v=nrm((DEC_BATCH, N_EVEN, DA_HEADS, PAST_LEN, HEAD_DIM)),
        cache_gq_k=nrm((DEC_BATCH, N_EVEN, GQ_KV_HEADS, PAST_LEN, HEAD_DIM)),
        cache_gq_v=nrm((DEC_BATCH, N_EVEN, GQ_KV_HEADS, PAST_LEN, HEAD_DIM)),
        cache_mla_ckv=nrm((DEC_BATCH, N_ODD, PAST_LEN, MLA_KV_RANK)),
        cache_mla_kpe=nrm((DEC_BATCH, N_ODD, PAST_LEN, MLA_ROPE)),
        state_ml_C=nrm((DEC_BATCH, N_ODD, 2, ML_HEADS, ML_DK, ML_DV)),
        state_ml_n=nrm((DEC_BATCH, N_ODD, 2, ML_HEADS, ML_DK)),
        state_ml_m=nrm((DEC_BATCH, N_ODD, 2, ML_HEADS)),
        c=nrm((DEC_BATCH, D)),
        c_ctx=nrm((D,)),
        norm1_g=gain((DEPTH, D)),
        norm2_g=gain((DEPTH, D)),
        w_mod=nrm((DEPTH, D, 6 * D), 0.5 * D ** -0.5),
        b_mod=nrm((DEPTH, 6 * D), 0.02),
        even_w_in=nrm((N_EVEN, D, EVEN_IN), D ** -0.5),
        even_w_out=nrm((N_EVEN, EVEN_MIX, D), EVEN_MIX ** -0.5),
        da_lambda=nrm((N_EVEN, 4, DA_SUB), 0.1),
        da_subln_g=gain((N_EVEN, HEAD_DIM)),
        gq_q_norm_g=gain((N_EVEN, HEAD_DIM)),
        gq_k_norm_g=gain((N_EVEN, HEAD_DIM)),
        odd_w_in=nrm((N_ODD, D, ODD_IN), D ** -0.5),
        odd_w_out=nrm((N_ODD, ODD_MIX, D), ODD_MIX ** -0.5),
        ml_conv_w=nrm((N_ODD, ML_CONV, 2 * ML_HEADS * ML_DK), ML_CONV ** -0.5),
        ml_conv_b=nrm((N_ODD, 2 * ML_HEADS * ML_DK), 0.02),
        ml_gate_b=gate_base[None, :] + nrm((N_ODD, 4 * ML_HEADS), 0.1),
        ml_norm_g=gain((N_ODD, ML_DV)),
        mla_q_norm_g=gain((N_ODD, MLA_Q_RANK)),
        mla_w_uq=nrm((N_ODD, MLA_Q_RANK, MLA_HEADS * (MLA_NOPE + MLA_ROPE)), MLA_Q_RANK ** -0.5),
        mla_kv_norm_g=gain((N_ODD, MLA_KV_RANK)),
        mla_w_ukv=nrm((N_ODD, MLA_KV_RANK, MLA_HEADS * (MLA_NOPE + MLA_V)), MLA_KV_RANK ** -0.5),
        ffn_w_up=nrm((DEPTH, D, 2 * D_FF), D ** -0.5),
        ffn_conv_w=nrm((DEPTH, FFN_CONV, 2 * D_FF), FFN_CONV ** -0.5),
        ffn_conv_b=nrm((DEPTH, 2 * D_FF), 0.02),
        ffn_w_down=nrm((DEPTH, D_FF, D), D_FF ** -0.5),
        final_norm_g=gain((D,)),
    )


def reference(x_prompt, x_sample, cache_da_k, cache_da_v, cache_gq_k, cache_gq_v,
              cache_mla_ckv, cache_mla_kpe, state_ml_C, state_ml_n, state_ml_m,
              c, c_ctx, norm1_g, norm2_g, w_mod, b_mod, even_w_in, even_w_out,
              da_lambda, da_subln_g, gq_q_norm_g, gq_k_norm_g, odd_w_in, odd_w_out,
              ml_conv_w, ml_conv_b, ml_gate_b, ml_norm_g, mla_q_norm_g, mla_w_uq,
              mla_kv_norm_g, mla_w_ukv, ffn_w_up, ffn_conv_w, ffn_conv_b, ffn_w_down,
              final_norm_g):
    grid_rows = x_sample.shape[1] // GRID_W
    cos_sub, sin_sub = _rope_tables(grid_rows, DA_SUB)
    cos_hd, sin_hd = _rope_tables(grid_rows, HEAD_DIM)
    cos_pe, sin_pe = _rope_tables(grid_rows, MLA_ROPE)
    b_ctx = x_prompt.shape[0]
    cond_ctx = c_ctx[None, :]
    xp, xs = x_prompt, x_sample
    da_k_l, da_v_l, gq_k_l, gq_v_l = [], [], [], []
    ckv_l, kpe_l, mC_l, mn_l, mm_l = [], [], [], [], []
    for l in range(DEPTH):
        mod_p = _adaln(cond_ctx, w_mod[l], b_mod[l])
        mod_s = _adaln(c, w_mod[l], b_mod[l])
        hp = _modulate(xp, norm1_g[l], mod_p[0], mod_p[1])
        hs = _modulate(xs, norm1_g[l], mod_s[0], mod_s[1])
        j = l // 2
        if l % 2 == 0:
            lam_init = 0.8 - 0.6 * math.exp(-0.3 * l)
            lq1, lk1, lq2, lk2 = da_lambda[j].astype(jnp.float32)
            lam = jnp.exp(jnp.sum(lq1 * lk1)) - jnp.exp(jnp.sum(lq2 * lk2)) + lam_init
            qa, ka, va, qb, kb, vb = _even_heads(hp, even_w_in[j], gq_q_norm_g[j], gq_k_norm_g[j])
            out_p = _even_merge(_diff_attention(qa, ka, va, lam, da_subln_g[j], lam_init),
                                _gqa(qb, kb, vb), even_w_out[j])
            da_k_l.append(ka)
            da_v_l.append(va)
            gq_k_l.append(kb)
            gq_v_l.append(vb)
            qa, ka, va, qb, kb, vb = _even_heads(hs, even_w_in[j], gq_q_norm_g[j], gq_k_norm_g[j])
            ka = jnp.concatenate([cache_da_k[:, j], _rope(ka, cos_sub, sin_sub)], axis=3)
            va = jnp.concatenate([cache_da_v[:, j], va], axis=2)
            kb = jnp.concatenate([cache_gq_k[:, j], _rope(kb, cos_hd, sin_hd)], axis=2)
            vb = jnp.concatenate([cache_gq_v[:, j], vb], axis=2)
            out_s = _even_merge(_diff_attention(_rope(qa, cos_sub, sin_sub), ka, va, lam, da_subln_g[j], lam_init),
                                _gqa(_rope(qb, cos_hd, sin_hd), kb, vb), even_w_out[j])
        else:
            qm, km, vm, om, gates, q_nope, q_pe, ckv, k_pe = _odd_heads(
                hp, odd_w_in[j], ml_conv_w[j], ml_conv_b[j], mla_q_norm_g[j], mla_w_uq[j], mla_kv_norm_g[j])
            zc = jnp.zeros((b_ctx, 2, ML_HEADS, ML_DK, ML_DV), jnp.float32)
            zn = jnp.zeros((b_ctx, 2, ML_HEADS, ML_DK), jnp.float32)
            zm = jnp.zeros((b_ctx, 2, ML_HEADS), jnp.float32)
            h_ml, (st_c, st_n, st_m) = _mlstm_mixer(qm, km, vm, om, gates, ml_gate_b[j], ml_norm_g[j], zc, zn, zm)
            k_nope, v = _mla_kv(ckv, mla_w_ukv[j])
            out_p = _odd_merge(h_ml, _mla_attention(q_nope, q_pe, k_nope, k_pe, v), odd_w_out[j])
            ckv_l.append(ckv)
            kpe_l.append(k_pe)
            mC_l.append(st_c)
            mn_l.append(st_n)
            mm_l.append(st_m)
            qm, km, vm, om, gates, q_nope, q_pe, ckv, k_pe = _odd_heads(
                hs, odd_w_in[j], ml_conv_w[j], ml_conv_b[j], mla_q_norm_g[j], mla_w_uq[j], mla_kv_norm_g[j])
            h_ml = _mlstm_mixer(qm, km, vm, om, gates, ml_gate_b[j], ml_norm_g[j],
                                state_ml_C[:, j], state_ml_n[:, j], state_ml_m[:, j])[0]
            ckv_all = jnp.concatenate([cache_mla_ckv[:, j], ckv], axis=1)
            kpe_all = jnp.concatenate([cache_mla_kpe[:, j], _rope(k_pe, cos_pe, sin_pe)], axis=1)
            k_nope, v = _mla_kv(ckv_all, mla_w_ukv[j])
            out_s = _odd_merge(h_ml, _mla_attention(q_nope, _rope(q_pe, cos_pe, sin_pe), k_nope, kpe_all, v),
                               odd_w_out[j])
        xp = xp + mod_p[2] * out_p
        xs = xs + mod_s[2] * out_s
        hp = _modulate(xp, norm2_g[l], mod_p[3], mod_p[4])
        hs = _modulate(xs, norm2_g[l], mod_s[3], mod_s[4])
        xp = xp + mod_p[5] * _conv_ffn(hp, ffn_w_up[l], ffn_conv_w[l], ffn_conv_b[l], ffn_w_down[l])
        xs = xs + mod_s[5] * _conv_ffn(hs, ffn_w_up[l], ffn_conv_w[l], ffn_conv_b[l], ffn_w_down[l])
    y_prompt = _rms(xp, final_norm_g)
    y_sample = _rms(xs, final_norm_g)
    new_da_k = jnp.stack(da_k_l, axis=1)
    new_da_v = jnp.stack(da_v_l, axis=1)
    new_gq_k = jnp.stack(gq_k_l, axis=1)
    new_gq_v = jnp.stack(gq_v_l, axis=1)
    new_mla_ckv = jnp.stack(ckv_l, axis=1)
    new_mla_kpe = jnp.stack(kpe_l, axis=1)
    new_ml_C = jnp.stack(mC_l, axis=1)
    new_ml_n = jnp.stack(mn_l, axis=1)
    new_ml_m = jnp.stack(mm_l, axis=1)
    return (y_prompt, y_sample, new_da_k, new_da_v, new_gq_k, new_gq_v, new_mla_ckv, new_mla_kpe, new_ml_C, new_ml_n, new_ml_m)
```

```python
import functools
import math

import jax
import jax.numpy as jnp
from jax import lax
from jax.experimental import pallas as pl
from jax.experimental.pallas import tpu as pltpu

F32 = jnp.float32
BF16 = jnp.bfloat16

EPS = 1e-6
GRID_W = 64
ROPE_THETA = 10000.0
HEAD_DIM = 128
DA_HEADS = 8
DA_SUB = HEAD_DIM // 2
GQ_HEADS = 8
GQ_KV_HEADS = 2
GQ_REP = GQ_HEADS // GQ_KV_HEADS
ML_HEADS = 8
ML_DK = 128
ML_DV = 128
MLA_HEADS = 8
MLA_Q_RANK = 512
MLA_KV_RANK = 256
MLA_NOPE = 128
MLA_ROPE = 64
MLA_V = 128

LANES = 128
MOD_ROWS = 8
VMEM_LIMIT = 56 * 1024 * 1024

ROW_TILE = 512
ML_CHUNK = 256


def _cparams(sem):
    return pltpu.CompilerParams(dimension_semantics=sem, vmem_limit_bytes=VMEM_LIMIT)


def _dot(a, b):
    return jnp.dot(a, b, preferred_element_type=F32)


def _dot_nt(a, b):
    return lax.dot_general(a, b, (((1,), (1,)), ((), ())), preferred_element_type=F32)


def _rms_rows(x, g):
    return x * lax.rsqrt(jnp.mean(x * x, axis=-1, keepdims=True) + EPS) * g


def _sigmoid(x):
    return 1.0 / (1.0 + jnp.exp(-x))


def _softmax_rows(s):
    m = jnp.max(s, axis=-1, keepdims=True)
    p = jnp.exp(s - m)
    return p / jnp.sum(p, axis=-1, keepdims=True)


def _rope_lanes(x, cos, sin_a, sin_b, shift):
    up = pltpu.roll(x, LANES - shift, axis=1)
    dn = pltpu.roll(x, shift, axis=1)
    return x * cos + up * sin_a + dn * sin_b


def _rope_tables(n_tok, dim):
    t = jnp.arange(n_tok)
    r = (t // GRID_W).astype(F32)
    col = (t % GRID_W).astype(F32)
    quarter = dim // 4
    inv = ROPE_THETA ** (-jnp.arange(quarter, dtype=F32) / quarter)
    ang = jnp.concatenate([r[:, None] * inv, r[:, None] * inv, col[:, None] * inv, col[:, None] * inv], axis=1)
    ang = jnp.tile(ang, (1, LANES // dim))
    lane = jnp.arange(LANES)
    lower = ((lane // quarter) % 2 == 0)[None, :]
    cos, sin = jnp.cos(ang), jnp.sin(ang)
    return cos, jnp.where(lower, -sin, 0.0), jnp.where(lower, 0.0, sin)


def _adaln_kernel(ct_ref, w_ref, b_ref, o_ref, *, n_cond, k_chunk):
    d = w_ref.shape[0]
    tn = w_ref.shape[1]

    def body(kc, accs):
        k0 = pl.multiple_of(kc * k_chunk, k_chunk)
        w = w_ref[pl.ds(k0, k_chunk), :]
        c = ct_ref[pl.ds(k0, k_chunk), :]
        s = c * _sigmoid(c)
        return tuple(acc + jnp.sum(w * s[:, r:r + 1], axis=0, keepdims=True) for r, acc in enumerate(accs))

    accs = lax.fori_loop(0, d // k_chunk, body, tuple(jnp.zeros((1, tn), F32) for _ in range(n_cond)))
    o_ref[...] = jnp.zeros(o_ref.shape, F32)
    for r in range(n_cond):
        o_ref[r:r + 1, :] = accs[r] + b_ref[...]


def _adaln(cond_t, w_mod, b_mod, n_cond):
    depth, d, n6 = w_mod.shape
    tn = 1024
    return pl.pallas_call(
        functools.partial(_adaln_kernel, n_cond=n_cond, k_chunk=256),
        out_shape=jax.ShapeDtypeStruct((depth, MOD_ROWS, n6), F32),
        grid=(depth, n6 // tn),
        in_specs=[
            pl.BlockSpec((d, LANES), lambda l, j: (0, 0)),
            pl.BlockSpec((None, d, tn), lambda l, j: (l, 0, j)),
            pl.BlockSpec((None, 1, tn), lambda l, j: (l, 0, j)),
        ],
        out_specs=pl.BlockSpec((None, MOD_ROWS, tn), lambda l, j: (l, 0, j)),
        compiler_params=_cparams(("parallel", "parallel")),
        name="adaln",
    )(cond_t, w_mod, b_mod.reshape(depth, 1, n6))


class _Rows:
    def __init__(self, bp, tp, bs, ts, tm):
        assert (bp * tp) % tm == 0 and ts % tm == 0
        self.n_prompt = bp * tp
        self.n_rows = bp * tp + bs * ts
        self.tm = tm
        self.prompt_tiles = self.n_prompt // tm
        self.tiles_per_sample = ts // tm

    def mod_row(self, i):
        return jnp.where(i < self.prompt_tiles, 0, 1 + (i - self.prompt_tiles) // self.tiles_per_sample)


def _norm_chunks(x_ref, h_ref, g_ref, sh_ref, sc_ref, aux_ref, rc):
    tm = x_ref.shape[0]

    def body(r, carry):
        r0 = pl.multiple_of(r * rc, rc)
        y = _rms_rows(x_ref[pl.ds(r0, rc), :], g_ref[...])
        if sc_ref is not None:
            y = y * (1.0 + sc_ref[0]) + sh_ref[0]
        if aux_ref is not None:
            aux_ref[pl.ds(r0, rc), :] = y
        h_ref[pl.ds(r0, rc), :] = y.astype(BF16)
        return carry

    lax.fori_loop(0, tm // rc, body, 0)


def _nmm_kernel(*refs, modulated, with_aux):
    it = iter(refs)
    x_ref, g_ref = next(it), next(it)
    sh_ref = sc_ref = None
    if modulated:
        sh_ref, sc_ref = next(it), next(it)
    w_ref, o_ref = next(it), next(it)
    aux_ref = next(it) if with_aux else None
    h_ref = next(it)

    @pl.when(pl.program_id(1) == 0)
    def _():
        _norm_chunks(x_ref, h_ref, g_ref, sh_ref, sc_ref, aux_ref, 128)

    o_ref[...] = _dot(h_ref[...], w_ref[...])


def _nmm(x, col_block, k, g, w, tn, rows=None, mods=None, layer=0, chunks=(0, 1), with_aux=False, tm=ROW_TILE):
    n_rows = x.shape[0]
    n = w.shape[1]
    in_specs = [pl.BlockSpec((tm, k), lambda i, j: (i, col_block)),
                pl.BlockSpec((1, k), lambda i, j: (0, 0))]
    args = [x, g.reshape(1, k)]
    if mods is not None:
        for c in chunks:
            in_specs.append(pl.BlockSpec(
                (1, 1, k), lambda i, j, c=c: (layer * MOD_ROWS + rows.mod_row(i), 0, c)))
            args.append(mods)
    in_specs.append(pl.BlockSpec((k, tn), lambda i, j: (0, j)))
    args.append(w)
    out_shape = [jax.ShapeDtypeStruct((n_rows, n), F32)]
    out_specs = [pl.BlockSpec((tm, tn), lambda i, j: (i, j))]
    if with_aux:
        out_shape.append(jax.ShapeDtypeStruct((n_rows, k), F32))
        out_specs.append(pl.BlockSpec((tm, k), lambda i, j: (i, 0)))
    res = pl.pallas_call(
        functools.partial(_nmm_kernel, modulated=mods is not None, with_aux=with_aux),
        out_shape=out_shape,
        grid=(n_rows // tm, n // tn),
        in_specs=in_specs,
        out_specs=out_specs,
        scratch_shapes=[pltpu.VMEM((tm, k), BF16)],
        compiler_params=_cparams(("parallel", "arbitrary")),
        name="norm_matmul",
    )(*args)
    return res if with_aux else res[0]


def _mm_kernel(a_ref, w_ref, o_ref):
    o_ref[...] = _dot(a_ref[...].astype(BF16), w_ref[...])


def _mm(a, w, tm, tn):
    m, k = a.shape
    n = w.shape[1]
    return pl.pallas_call(
        _mm_kernel,
        out_shape=jax.ShapeDtypeStruct((m, n), F32),
        grid=(m // tm, n // tn),
        in_specs=[pl.BlockSpec((tm, k), lambda i, j: (i, 0)),
                  pl.BlockSpec((k, tn), lambda i, j: (0, j))],
        out_specs=pl.BlockSpec((tm, tn), lambda i, j: (i, j)),
        compiler_params=_cparams(("parallel", "parallel")),
        name="matmul",
    )(a, w)


def _proj_res_kernel(a1_ref, a2_ref, w1_ref, w2_ref, x_ref, gate_ref, o_ref):
    acc = _dot(a1_ref[...], w1_ref[...]) + _dot(a2_ref[...], w2_ref[...])
    o_ref[...] = x_ref[...] + gate_ref[0] * acc


def _proj_res(a1, a2, w, x, mods, rows, layer, chunk, tn=512, tm=ROW_TILE):
    n_rows, k1 = a1.shape
    n = w.shape[1]
    per = n // tn
    return pl.pallas_call(
        _proj_res_kernel,
        out_shape=jax.ShapeDtypeStruct((n_rows, n), F32),
        grid=(n_rows // tm, n // tn),
        in_specs=[
            pl.BlockSpec((tm, k1), lambda i, j: (i, 0)),
            pl.BlockSpec((tm, k1), lambda i, j: (i, 0)),
            pl.BlockSpec((k1, tn), lambda i, j: (0, j)),
            pl.BlockSpec((k1, tn), lambda i, j: (1, j)),
            pl.BlockSpec((tm, tn), lambda i, j: (i, j)),
            pl.BlockSpec((1, 1, tn), lambda i, j: (layer * MOD_ROWS + rows.mod_row(i), 0, chunk * per + j)),
        ],
        out_specs=pl.BlockSpec((tm, tn), lambda i, j: (i, j)),
        compiler_params=_cparams(("parallel", "parallel")),
        name="proj_residual",
    )(a1, a2, w, w, x, mods)


def _ffn_kernel(x_ref, xp_ref, xn_ref, g_ref, sh_ref, sc_ref, gate_ref, kp_ref, kn_ref,
                wg_ref, wu_ref, cwg_ref, cwu_ref, cbg_ref, cbu_ref, wd_ref, o_ref,
                h_ref, hh_ref, acc_ref):
    j = pl.program_id(1)
    tm = x_ref.shape[0]

    @pl.when(j == 0)
    def _():
        _norm_chunks(x_ref, h_ref, g_ref, sh_ref, sc_ref, None, 128)
        halo = jnp.concatenate([xp_ref[...], xn_ref[...]], axis=0)
        y = _rms_rows(halo, g_ref[...]) * (1.0 + sc_ref[0]) + sh_ref[0]
        hh_ref[...] = y.astype(BF16)
        acc_ref[...] = jnp.zeros(acc_ref.shape, F32)

    row = lax.broadcasted_iota(jnp.int32, (tm, 1), 0)
    keep_prev = kp_ref[...] > 0.0
    keep_next = kn_ref[...] > 0.0

    def conv(w_ref, cw_ref, cb_ref):
        u = _dot(h_ref[...], w_ref[...])
        uh = _dot(hh_ref[...], w_ref[...])
        prev = jnp.where(row == 0, uh[7:8, :], pltpu.roll(u, 1, axis=0))
        prev = jnp.where(keep_prev, prev, 0.0)
        nxt = jnp.where(row == tm - 1, uh[8:9, :], pltpu.roll(u, tm - 1, axis=0))
        nxt = jnp.where(keep_next, nxt, 0.0)
        return cb_ref[...] + prev * cw_ref[0:1, :] + u * cw_ref[1:2, :] + nxt * cw_ref[2:3, :]

    cg = conv(wg_ref, cwg_ref, cbg_ref)
    cu = conv(wu_ref, cwu_ref, cbu_ref)
    act = (cg * _sigmoid(cg) * cu).astype(BF16)
    acc_ref[...] += _dot(act, wd_ref[...])

    @pl.when(j == pl.num_programs(1) - 1)
    def _():
        o_ref[...] = x_ref[...] + gate_ref[0] * acc_ref[...]


def _ffn(x, g, mods, rows, layer, w_up, conv_w, conv_b, w_down, keep_prev, keep_next, tn=512, tm=ROW_TILE):
    n_rows, d = x.shape
    d_ff = w_down.shape[0]
    nj = d_ff // tn
    last8 = n_rows // 8 - 1

    def mod_spec(c):
        return pl.BlockSpec((1, 1, d), lambda i, j: (layer * MOD_ROWS + rows.mod_row(i), 0, c))

    return pl.pallas_call(
        _ffn_kernel,
        out_shape=jax.ShapeDtypeStruct((n_rows, d), F32),
        grid=(n_rows // tm, nj),
        in_specs=[
            pl.BlockSpec((tm, d), lambda i, j: (i, 0)),
            pl.BlockSpec((8, d), lambda i, j: (jnp.maximum(i * (tm // 8) - 1, 0), 0)),
            pl.BlockSpec((8, d), lambda i, j: (jnp.minimum((i + 1) * (tm // 8), last8), 0)),
            pl.BlockSpec((1, d), lambda i, j: (0, 0)),
            mod_spec(3), mod_spec(4), mod_spec(5),
            pl.BlockSpec((tm, 1), lambda i, j: (i, 0)),
            pl.BlockSpec((tm, 1), lambda i, j: (i, 0)),
            pl.BlockSpec((d, tn), lambda i, j: (0, j)),
            pl.BlockSpec((d, tn), lambda i, j: (0, nj + j)),
            pl.BlockSpec((3, tn), lambda i, j: (0, j)),
            pl.BlockSpec((3, tn), lambda i, j: (0, nj + j)),
            pl.BlockSpec((1, tn), lambda i, j: (0, j)),
            pl.BlockSpec((1, tn), lambda i, j: (0, nj + j)),
            pl.BlockSpec((tn, d), lambda i, j: (j, 0)),
        ],
        out_specs=pl.BlockSpec((tm, d), lambda i, j: (i, 0)),
        scratch_shapes=[pltpu.VMEM((tm, d), BF16), pltpu.VMEM((16, d), BF16), pltpu.VMEM((tm, d), F32)],
        compiler_params=_cparams(("parallel", "arbitrary")),
        name="conv_ffn",
    )(x, x, x, g.reshape(1, d), mods, mods, mods, keep_prev, keep_next,
      w_up, w_up, conv_w, conv_w, conv_b.reshape(1, -1), conv_b.reshape(1, -1), w_down)


def _final_norm_kernel(x_ref, g_ref, o_ref):
    o_ref[...] = _rms_rows(x_ref[...], g_ref[...])


def _final_norm(x, g, tm=256):
    n_rows, d = x.shape
    return pl.pallas_call(
        _final_norm_kernel,
        out_shape=jax.ShapeDtypeStruct((n_rows, d), F32),
        grid=(n_rows // tm,),
        in_specs=[pl.BlockSpec((tm, d), lambda i: (i, 0)), pl.BlockSpec((1, d), lambda i: (0, 0))],
        out_specs=pl.BlockSpec((tm, d), lambda i: (i, 0)),
        compiler_params=_cparams(("parallel",)),
        name="final_norm",
    )(x, g.reshape(1, d))


def _even_attn_kernel(*refs, past, use_rope, emit_cache, lam_init):
    it = iter(refs)
    qa_ref, qb_ref = next(it), next(it)
    ka_ref, va_ref, kb_ref, vb_ref = next(it), next(it), next(it), next(it)
    lam_ref, subln_ref, qn_ref, kn_ref = next(it), next(it), next(it), next(it)
    if past:
        cdak_ref, cdav_ref, cgqk_ref, cgqv_ref = next(it), next(it), next(it), next(it)
    if use_rope:
        q64_ref, k64_ref, q128_ref, k128_ref = next(it), next(it), next(it), next(it)
    oa_ref, ob_ref = next(it), next(it)
    if emit_cache:
        dak_ref, dav_ref, gqk_ref, gqv_ref = next(it), next(it), next(it), next(it)
    kda_s, vda_s, kgq_s, vgq_s = next(it), next(it), next(it), next(it)

    t_len = ka_ref.shape[0]

    @pl.when(pl.program_id(1) == 0)
    def _():
        for h in range(DA_HEADS):
            cols = slice(h * HEAD_DIM, (h + 1) * HEAD_DIM)
            k = ka_ref[:, cols]
            v = va_ref[:, cols]
            if emit_cache:
                dak_ref[h, 0] = k[:, :DA_SUB]
                dak_ref[h, 1] = k[:, DA_SUB:]
                dav_ref[h] = v
            if use_rope:
                k = _rope_lanes(k, k64_ref[0], k64_ref[1], k64_ref[2], DA_SUB // 4)
            kda_s[h, 0, past:, :] = k[:, :DA_SUB].astype(BF16)
            kda_s[h, 1, past:, :] = k[:, DA_SUB:].astype(BF16)
            vda_s[h, past:, :] = v.astype(BF16)
            if past:
                kda_s[h, 0, :past, :] = cdak_ref[h, 0].astype(BF16)
                kda_s[h, 1, :past, :] = cdak_ref[h, 1].astype(BF16)
                vda_s[h, :past, :] = cdav_ref[h].astype(BF16)
        for g in range(GQ_KV_HEADS):
            cols = slice(g * HEAD_DIM, (g + 1) * HEAD_DIM)
            k = _rms_rows(kb_ref[:, cols], kn_ref[...])
            v = vb_ref[:, cols]
            if emit_cache:
                gqk_ref[g] = k
                gqv_ref[g] = v
            if use_rope:
                k = _rope_lanes(k, k128_ref[0], k128_ref[1], k128_ref[2], HEAD_DIM // 4)
            kgq_s[g, past:, :] = k.astype(BF16)
            vgq_s[g, past:, :] = v.astype(BF16)
            if past:
                kgq_s[g, :past, :] = cgqk_ref[g].astype(BF16)
                vgq_s[g, :past, :] = cgqv_ref[g].astype(BF16)

    lam4 = lam_ref[...]
    lam = (jnp.exp(jnp.sum(lam4[0:1] * lam4[1:2], axis=-1, keepdims=True))
           - jnp.exp(jnp.sum(lam4[2:3] * lam4[3:4], axis=-1, keepdims=True)) + lam_init)

    da_scale = DA_SUB ** -0.5
    for h in range(DA_HEADS):
        cols = slice(h * HEAD_DIM, (h + 1) * HEAD_DIM)
        q = qa_ref[:, cols]
        if use_rope:
            q = _rope_lanes(q, q64_ref[0], q64_ref[1], q64_ref[2], DA_SUB // 4)
        a1 = _softmax_rows(_dot_nt(q[:, :DA_SUB].astype(BF16), kda_s[h, 0]) * da_scale)
        a2 = _softmax_rows(_dot_nt(q[:, DA_SUB:].astype(BF16), kda_s[h, 1]) * da_scale)
        o = _dot((a1 - lam * a2).astype(BF16), vda_s[h])
        oa_ref[:, cols] = (_rms_rows(o, subln_ref[...]) * (1.0 - lam_init)).astype(BF16)

    gq_scale = HEAD_DIM ** -0.5
    for h in range(GQ_HEADS):
        cols = slice(h * HEAD_DIM, (h + 1) * HEAD_DIM)
        q = _rms_rows(qb_ref[:, cols], qn_ref[...])
        if use_rope:
            q = _rope_lanes(q, q128_ref[0], q128_ref[1], q128_ref[2], HEAD_DIM // 4)
        a = _softmax_rows(_dot_nt(q.astype(BF16), kgq_s[h // GQ_REP]) * gq_scale)
        ob_ref[:, cols] = _dot(a.astype(BF16), vgq_s[h // GQ_REP]).astype(BF16)


def _even_attn(proj, row0, nb, t_len, tq, lam, subln_g, qn_g, kn_g, lam_init,
               caches=None, ropes=None, emit_cache=False):
    n_rows = proj.shape[0]
    hd = DA_HEADS * HEAD_DIM
    kvw = GQ_KV_HEADS * HEAD_DIM
    nq = t_len // tq
    qb0 = row0 // tq
    kb0 = row0 // t_len
    past = caches[0].shape[3] if caches is not None else 0
    s_len = past + t_len

    def qspec(cb):
        return pl.BlockSpec((tq, hd), lambda b, q: (qb0 + b * nq + q, cb))

    def kspec(width, cb):
        return pl.BlockSpec((t_len, width), lambda b, q: (kb0 + b, cb))

    def const(shape):
        return pl.BlockSpec(shape, lambda b, q: (0,) * len(shape))

    in_specs = [qspec(0), qspec(3), kspec(hd, 1), kspec(hd, 2), kspec(kvw, 4 * hd // kvw), kspec(kvw, 4 * hd // kvw + 1),
                const((4, DA_SUB)), const((1, HEAD_DIM)), const((1, HEAD_DIM)), const((1, HEAD_DIM))]
    args = [proj, proj, proj, proj, proj, proj, lam, subln_g.reshape(1, -1), qn_g.reshape(1, -1), kn_g.reshape(1, -1)]
    if caches is not None:
        cdak, cdav, cgqk, cgqv = caches
        in_specs += [
            pl.BlockSpec((None, DA_HEADS, 2, past, DA_SUB), lambda b, q: (b, 0, 0, 0, 0)),
            pl.BlockSpec((None, DA_HEADS, past, HEAD_DIM), lambda b, q: (b, 0, 0, 0)),
            pl.BlockSpec((None, GQ_KV_HEADS, past, HEAD_DIM), lambda b, q: (b, 0, 0, 0)),
            pl.BlockSpec((None, GQ_KV_HEADS, past, HEAD_DIM), lambda b, q: (b, 0, 0, 0)),
        ]
        args += [cdak, cdav, cgqk, cgqv]
    if ropes is not None:
        r64, r128 = ropes
        in_specs += [pl.BlockSpec((3, tq, LANES), lambda b, q: (0, q, 0)),
                     pl.BlockSpec((3, t_len, LANES), lambda b, q: (0, 0, 0)),
                     pl.BlockSpec((3, tq, LANES), lambda b, q: (0, q, 0)),
                     pl.BlockSpec((3, t_len, LANES), lambda b, q: (0, 0, 0))]
        args += [r64, r64, r128, r128]
    out_shape = [jax.ShapeDtypeStruct((nb * t_len, hd), BF16), jax.ShapeDtypeStruct((nb * t_len, hd), BF16)]
    out_specs = [pl.BlockSpec((tq, hd), lambda b, q: (b * nq + q, 0)),
                 pl.BlockSpec((tq, hd), lambda b, q: (b * nq + q, 0))]
    if emit_cache:
        out_shape += [jax.ShapeDtypeStruct((nb, DA_HEADS, 2, t_len, DA_SUB), F32),
                      jax.ShapeDtypeStruct((nb, DA_HEADS, t_len, HEAD_DIM), F32),
                      jax.ShapeDtypeStruct((nb, GQ_KV_HEADS, t_len, HEAD_DIM), F32),
                      jax.ShapeDtypeStruct((nb, GQ_KV_HEADS, t_len, HEAD_DIM), F32)]
        out_specs += [pl.BlockSpec((None, DA_HEADS, 2, t_len, DA_SUB), lambda b, q: (b, 0, 0, 0, 0)),
                      pl.BlockSpec((None, DA_HEADS, t_len, HEAD_DIM), lambda b, q: (b, 0, 0, 0)),
                      pl.BlockSpec((None, GQ_KV_HEADS, t_len, HEAD_DIM), lambda b, q: (b, 0, 0, 0)),
                      pl.BlockSpec((None, GQ_KV_HEADS, t_len, HEAD_DIM), lambda b, q: (b, 0, 0, 0))]
    return pl.pallas_call(
        functools.partial(_even_attn_kernel, past=past, use_rope=ropes is not None,
                          emit_cache=emit_cache, lam_init=lam_init),
        out_shape=out_shape,
        grid=(nb, nq),
        in_specs=in_specs,
        out_specs=out_specs,
        scratch_shapes=[pltpu.VMEM((DA_HEADS, 2, s_len, DA_SUB), BF16),
                        pltpu.VMEM((DA_HEADS, s_len, HEAD_DIM), BF16),
                        pltpu.VMEM((GQ_KV_HEADS, s_len, HEAD_DIM), BF16),
                        pltpu.VMEM((GQ_KV_HEADS, s_len, HEAD_DIM), BF16)],
        compiler_params=_cparams(("parallel", "arbitrary")),
        name="even_attention",
    )(*args)


def _mlstm_chunk(qc, kc, vc, lf_r, lf_c, li_r, li_c, state, reverse):
    n = qc.shape[0]
    ti = lax.broadcasted_iota(jnp.int32, (n, n), 0)
    si = lax.broadcasted_iota(jnp.int32, (n, n), 1)
    seen = (si >= ti) if reverse else (si <= ti)
    b_c = jnp.sum(jnp.where(seen, lf_r, 0.0), axis=1, keepdims=True)
    seen_t = (ti >= si) if reverse else (ti <= si)
    b_r = jnp.sum(jnp.where(seen_t, lf_c, 0.0), axis=0, keepdims=True)
    b_end = b_r[:, 0:1] if reverse else b_r[:, n - 1:n]
    d_intra = jnp.where(seen, b_c - b_r + li_r, -jnp.inf)
    row_max = jnp.max(d_intra, axis=1, keepdims=True)
    qb, kb, vb = qc.astype(BF16), kc.astype(BF16), vc.astype(BF16)
    if state is None:
        m_prev = jnp.zeros((1, 1), F32)
    else:
        c_prev, n_prev, m_prev = state
    d_inter = b_c + m_prev
    m_t = jnp.maximum(d_inter, row_max)
    s = _dot_nt(qb, kb) * jnp.exp(d_intra - m_t)
    num = _dot(s.astype(BF16), vb)
    den = jnp.sum(s, axis=1, keepdims=True)
    if state is not None:
        a = jnp.exp(d_inter - m_t)
        num = num + a * _dot(qb, c_prev.astype(BF16))
        den = den + a * jnp.sum(qc * n_prev, axis=1, keepdims=True)
    h = num / jnp.maximum(jnp.abs(den), jnp.exp(-m_t))
    g_c = b_end - b_c + li_c
    g_r = b_end - b_r + li_r
    m_new = jnp.maximum(b_end + m_prev, jnp.max(g_r, axis=1, keepdims=True))
    kw = kc * jnp.exp(g_c - m_new)
    c_new = _dot(kw.T.astype(BF16), vb)
    n_new = jnp.sum(kw, axis=0, keepdims=True)
    if state is not None:
        decay = jnp.exp(b_end + m_prev - m_new)
        c_new = decay * c_prev + c_new
        n_new = decay * n_prev + n_new
    return h, (c_new, n_new, m_new)


def _mlstm_kernel(*refs, has_state, emit_state, chunk):
    it = iter(refs)
    q_ref, k_ref, v_ref, o_ref, gt_ref = next(it), next(it), next(it), next(it), next(it)
    cwq_ref, cwk_ref, cbq_ref, cbk_ref, gb_ref, ng_ref = next(it), next(it), next(it), next(it), next(it), next(it)
    if has_state:
        c0_ref, n0_ref, m0_ref = next(it), next(it), next(it)
    mix_ref = next(it)
    if emit_state:
        co_ref, no_ref, mo_ref = next(it), next(it), next(it)
    hacc = next(it)

    t_len = q_ref.shape[0]
    head = pl.program_id(1)
    row = lax.broadcasted_iota(jnp.int32, (t_len, 1), 0)

    def conv_silu(x_ref, w_ref, b_ref):
        x = x_ref[...]
        prev = jnp.where(row == 0, 0.0, pltpu.roll(x, 1, axis=0))
        nxt = jnp.where(row == t_len - 1, 0.0, pltpu.roll(x, t_len - 1, axis=0))
        y = b_ref[...] + prev * w_ref[0:1, :] + x * w_ref[1:2, :] + nxt * w_ref[2:3, :]
        return y * _sigmoid(y)

    q = conv_silu(q_ref, cwq_ref, cbq_ref) * (ML_DK ** -0.5)
    k = conv_silu(k_ref, cwk_ref, cbk_ref)
    v = v_ref[...]

    gates = gt_ref[...] + gb_ref[...]
    gates_t = gates.T
    lane = lax.broadcasted_iota(jnp.int32, gates.shape, 1)
    sub = lax.broadcasted_iota(jnp.int32, gates_t.shape, 0)

    def gate_col(kind):
        return jnp.sum(jnp.where(lane == MLA_ROPE + kind * ML_HEADS + head, gates, 0.0), axis=1, keepdims=True)

    def gate_row(kind):
        return jnp.sum(jnp.where(sub == MLA_ROPE + kind * ML_HEADS + head, gates_t, 0.0), axis=0, keepdims=True)

    def log_sigmoid(x):
        return jnp.minimum(x, 0.0) - jnp.log(1.0 + jnp.exp(-jnp.abs(x)))

    n_chunks = t_len // chunk
    for d in range(2):
        li_c, li_r = gate_col(2 * d), gate_row(2 * d)
        lf_c, lf_r = log_sigmoid(gate_col(2 * d + 1)), log_sigmoid(gate_row(2 * d + 1))
        state = (c0_ref[d], n0_ref[d], m0_ref[d][:, 0:1]) if has_state else None
        order = range(n_chunks) if d == 0 else range(n_chunks - 1, -1, -1)
        for c in order:
            sl = slice(c * chunk, (c + 1) * chunk)
            h, state = _mlstm_chunk(q[sl], k[sl], v[sl], lf_r[:, sl], lf_c[sl], li_r[:, sl], li_c[sl],
                                    state, reverse=(d == 1))
            if d == 0:
                hacc[sl, :] = h
            else:
                hacc[sl, :] += h
        if emit_state:
            co_ref[d] = state[0]
            no_ref[d] = state[1]
            mo_ref[d] = jnp.broadcast_to(state[2], (1, LANES))

    hn = _rms_rows(hacc[...], ng_ref[...])
    mix_ref[...] = (hn * _sigmoid(o_ref[...])).astype(BF16)


def _mlstm(proj, row0, nb, t_len, conv_w, conv_b, gate_b_lanes, norm_g, states=None, emit_state=False):
    n_rows = proj.shape[0]
    hw = ML_HEADS * ML_DK
    kb0 = row0 // t_len
    nh = ML_HEADS
    chunk = min(ML_CHUNK, t_len)
    assert t_len % chunk == 0
    gate_block = (4 * hw + MLA_Q_RANK + MLA_KV_RANK) // LANES

    def tok(cb):
        return pl.BlockSpec((t_len, ML_DK), lambda b, h: (kb0 + b, cb * nh + h))

    in_specs = [tok(0), tok(1), tok(2), tok(3),
                pl.BlockSpec((t_len, LANES), lambda b, h: (kb0 + b, gate_block)),
                pl.BlockSpec((3, ML_DK), lambda b, h: (0, h)),
                pl.BlockSpec((3, ML_DK), lambda b, h: (0, nh + h)),
                pl.BlockSpec((1, ML_DK), lambda b, h: (0, h)),
                pl.BlockSpec((1, ML_DK), lambda b, h: (0, nh + h)),
                pl.BlockSpec((1, LANES), lambda b, h: (0, 0)),
                pl.BlockSpec((1, ML_DV), lambda b, h: (0, 0))]
    args = [proj, proj, proj, proj, proj, conv_w, conv_w, conv_b.reshape(1, -1), conv_b.reshape(1, -1),
            gate_b_lanes, norm_g.reshape(1, -1)]
    if states is not None:
        c0, n0, m0 = states
        in_specs += [pl.BlockSpec((None, 2, None, ML_DK, ML_DV), lambda b, h: (b, 0, h, 0, 0)),
                     pl.BlockSpec((None, 2, None, 1, ML_DK), lambda b, h: (b, 0, h, 0, 0)),
                     pl.BlockSpec((None, 2, None, 1, LANES), lambda b, h: (b, 0, h, 0, 0))]
        args += [c0, n0, m0]
    out_shape = [jax.ShapeDtypeStruct((nb * t_len, hw), BF16)]
    out_specs = [pl.BlockSpec((t_len, ML_DV), lambda b, h: (b, h))]
    if emit_state:
        out_shape += [jax.ShapeDtypeStruct((nb, 2, nh, ML_DK, ML_DV), F32),
                      jax.ShapeDtypeStruct((nb, 2, nh, 1, ML_DK), F32),
                      jax.ShapeDtypeStruct((nb, 2, nh, 1, LANES), F32)]
        out_specs += [pl.BlockSpec((None, 2, None, ML_DK, ML_DV), lambda b, h: (b, 0, h, 0, 0)),
                      pl.BlockSpec((None, 2, None, 1, ML_DK), lambda b, h: (b, 0, h, 0, 0)),
                      pl.BlockSpec((None, 2, None, 1, LANES), lambda b, h: (b, 0, h, 0, 0))]
    return pl.pallas_call(
        functools.partial(_mlstm_kernel, has_state=states is not None, emit_state=emit_state, chunk=chunk),
        out_shape=out_shape,
        grid=(nb, nh),
        in_specs=in_specs,
        out_specs=out_specs,
        scratch_shapes=[pltpu.VMEM((t_len, ML_DV), F32)],
        compiler_params=_cparams(("parallel", "parallel")),
        name="mlstm",
    )(*args)


def _mla_attn_kernel(*refs, past, use_rope):
    it = iter(refs)
    cq_ref, kv_ref, kpe_ref = next(it), next(it), next(it)
    if past:
        kvc_ref, kpec_ref = next(it), next(it)
    if use_rope:
        q64_ref, k64_ref = next(it), next(it)
    o_ref = next(it)
    kn_s, v_s, kpe_s = next(it), next(it), next(it)

    nope_w = MLA_HEADS * MLA_NOPE

    @pl.when(pl.program_id(1) == 0)
    def _():
        for h in range(MLA_HEADS):
            c0 = h * (MLA_NOPE + MLA_V)
            kn_s[h, past:, :] = kv_ref[:, c0:c0 + MLA_NOPE].astype(BF16)
            v_s[h, past:, :] = kv_ref[:, c0 + MLA_NOPE:c0 + MLA_NOPE + MLA_V].astype(BF16)
            if past:
                kn_s[h, :past, :] = kvc_ref[:, c0:c0 + MLA_NOPE].astype(BF16)
                v_s[h, :past, :] = kvc_ref[:, c0 + MLA_NOPE:c0 + MLA_NOPE + MLA_V].astype(BF16)
        kpe = kpe_ref[...]
        if use_rope:
            kpe = _rope_lanes(kpe, k64_ref[0], k64_ref[1], k64_ref[2], MLA_ROPE // 4)
        kpe_s[past:, :] = kpe[:, :MLA_ROPE].astype(BF16)
        if past:
            kpe_s[:past, :] = kpec_ref[...].astype(BF16)

    scale = (MLA_NOPE + MLA_ROPE) ** -0.5
    for h2 in range(MLA_HEADS // 2):
        qp2 = cq_ref[:, nope_w + h2 * LANES:nope_w + (h2 + 1) * LANES]
        if use_rope:
            qp2 = _rope_lanes(qp2, q64_ref[0], q64_ref[1], q64_ref[2], MLA_ROPE // 4)
        for e in range(2):
            h = 2 * h2 + e
            qn = cq_ref[:, h * MLA_NOPE:(h + 1) * MLA_NOPE].astype(BF16)
            qp = qp2[:, e * MLA_ROPE:(e + 1) * MLA_ROPE].astype(BF16)
            s = (_dot_nt(qn, kn_s[h]) + _dot_nt(qp, kpe_s[...])) * scale
            a = _softmax_rows(s)
            o_ref[:, h * MLA_V:(h + 1) * MLA_V] = _dot(a.astype(BF16), v_s[h]).astype(BF16)


def _mla_attn(cq, kv, proj, row0, nb, t_len, tq, cache=None, rope64=None):
    n_rows = cq.shape[0]
    nq = t_len // tq
    qb0 = row0 // tq
    kb0 = row0 // t_len
    past = cache[0].shape[0] // nb if cache is not None else 0
    s_len = past + t_len
    kpe_block = (4 * ML_HEADS * ML_DK + MLA_Q_RANK + MLA_KV_RANK) // LANES
    in_specs = [pl.BlockSpec((tq, cq.shape[1]), lambda b, q: (qb0 + b * nq + q, 0)),
                pl.BlockSpec((t_len, kv.shape[1]), lambda b, q: (kb0 + b, 0)),
                pl.BlockSpec((t_len, LANES), lambda b, q: (kb0 + b, kpe_block))]
    args = [cq, kv, proj]
    if cache is not None:
        kvc, kpec = cache
        in_specs += [pl.BlockSpec((past, kvc.shape[1]), lambda b, q: (b, 0)),
                     pl.BlockSpec((None, past, MLA_ROPE), lambda b, q: (b, 0, 0))]
        args += [kvc, kpec]
    if rope64 is not None:
        in_specs += [pl.BlockSpec((3, tq, LANES), lambda b, q: (0, q, 0)),
                     pl.BlockSpec((3, t_len, LANES), lambda b, q: (0, 0, 0))]
        args += [rope64, rope64]
    return pl.pallas_call(
        functools.partial(_mla_attn_kernel, past=past, use_rope=rope64 is not None),
        out_shape=jax.ShapeDtypeStruct((nb * t_len, MLA_HEADS * MLA_V), BF16),
        grid=(nb, nq),
        in_specs=in_specs,
        out_specs=pl.BlockSpec((tq, MLA_HEADS * MLA_V), lambda b, q: (b * nq + q, 0)),
        scratch_shapes=[pltpu.VMEM((MLA_HEADS, s_len, MLA_NOPE), BF16),
                        pltpu.VMEM((MLA_HEADS, s_len, MLA_V), BF16),
                        pltpu.VMEM((s_len, MLA_ROPE), BF16)],
        compiler_params=_cparams(("parallel", "arbitrary")),
        name="mla_attention",
    )(*args)


def _merge_rows(prompt_part, sample_part):
    return jnp.concatenate([prompt_part, sample_part], axis=0)


def kernel(x_prompt, x_sample, cache_da_k, cache_da_v, cache_gq_k, cache_gq_v, cache_mla_ckv, cache_mla_kpe, state_ml_C, state_ml_n, state_ml_m, c, c_ctx, norm1_g, norm2_g, w_mod, b_mod, even_w_in, even_w_out, da_lambda, da_subln_g, gq_q_norm_g, gq_k_norm_g, odd_w_in, odd_w_out, ml_conv_w, ml_conv_b, ml_gate_b, ml_norm_g, mla_q_norm_g, mla_w_uq, mla_kv_norm_g, mla_w_ukv, ffn_w_up, ffn_conv_w, ffn_conv_b, ffn_w_down, final_norm_g):
    bp, tp, d = x_prompt.shape
    bs, ts, _ = x_sample.shape
    depth = norm1_g.shape[0]
    past = cache_da_k.shape[4]
    n_prompt = bp * tp
    rows = _Rows(bp, tp, bs, ts, ROW_TILE)
    n_cond = 1 + bs
    assert n_cond <= MOD_ROWS

    x = jnp.concatenate([x_prompt.reshape(n_prompt, d), x_sample.reshape(bs * ts, d)], axis=0)

    cond_t = jnp.zeros((d, LANES), F32).at[:, :n_cond].set(jnp.concatenate([c_ctx[None, :], c], axis=0).T)
    mods = _adaln(cond_t, w_mod, b_mod, n_cond).reshape(depth * MOD_ROWS, 1, 6 * d)

    pos = jnp.concatenate([jnp.tile(jnp.arange(tp), bp), jnp.tile(jnp.arange(ts), bs)])
    seq_len = jnp.concatenate([jnp.full((n_prompt,), tp), jnp.full((bs * ts,), ts)])
    keep_prev = (pos != 0).astype(F32)[:, None]
    keep_next = (pos != seq_len - 1).astype(F32)[:, None]

    rope64 = jnp.stack(_rope_tables(ts, DA_SUB))
    rope128 = jnp.stack(_rope_tables(ts, HEAD_DIM))

    tq_s = min(256, ts)
    new_da_k, new_da_v, new_gq_k, new_gq_v = [], [], [], []
    new_ckv, new_kpe, new_c, new_n, new_m = [], [], [], [], []

    for l in range(depth):
        j = l // 2
        if l % 2 == 0:
            lam_init = 0.8 - 0.6 * math.exp(-0.3 * l)
            proj = _nmm(x, 0, d, norm1_g[l], even_w_in[j].astype(BF16), 512, rows=rows, mods=mods, layer=l)
            common = (da_lambda[j], da_subln_g[j], gq_q_norm_g[j], gq_k_norm_g[j], lam_init)
            oa_p, ob_p, dak, dav, gqk, gqv = _even_attn(proj, 0, bp, tp, tp, *common, emit_cache=True)
            oa_s, ob_s = _even_attn(proj, n_prompt, bs, ts, tq_s, *common,
                                    caches=(cache_da_k[:, j], cache_da_v[:, j], cache_gq_k[:, j], cache_gq_v[:, j]),
                                    ropes=(rope64, rope128))
            new_da_k.append(dak)
            new_da_v.append(dav)
            new_gq_k.append(gqk)
            new_gq_v.append(gqv)
            mix_a = _merge_rows(oa_p, oa_s)
            mix_b = _merge_rows(ob_p, ob_s)
            w_out = even_w_out[j].astype(BF16)
        else:
            hw = ML_HEADS * ML_DK
            w_in = odd_w_in[j]
            o_gate = 4 * hw
            o_qd = o_gate + 4 * ML_HEADS
            o_kvd = o_qd + MLA_Q_RANK
            o_kpe = o_kvd + MLA_KV_RANK
            n_in = w_in.shape[1]
            n_pad = -(-(n_in) // 512) * 512
            w_perm = jnp.concatenate([w_in[:, :o_gate], w_in[:, o_qd:o_kvd], w_in[:, o_kvd:o_kpe], w_in[:, o_kpe:],
                                      w_in[:, o_gate:o_qd], jnp.zeros((d, n_pad - n_in), F32)], axis=1).astype(BF16)
            proj = _nmm(x, 0, d, norm1_g[l], w_perm, 512, rows=rows, mods=mods, layer=l)
            gate_b_lanes = jnp.zeros((1, LANES), F32).at[0, MLA_ROPE:MLA_ROPE + 4 * ML_HEADS].set(ml_gate_b[j])

            ml_p, st_c, st_n, st_m = _mlstm(proj, 0, bp, tp, ml_conv_w[j], ml_conv_b[j], gate_b_lanes,
                                            ml_norm_g[j], emit_state=True)
            m0 = jnp.broadcast_to(state_ml_m[:, j][..., None, None], (bs, 2, ML_HEADS, 1, LANES))
            (ml_s,) = _mlstm(proj, n_prompt, bs, ts, ml_conv_w[j], ml_conv_b[j], gate_b_lanes, ml_norm_g[j],
                             states=(state_ml_C[:, j], state_ml_n[:, j][:, :, :, None, :], m0))
            new_c.append(st_c)
            new_n.append(st_n[:, :, :, 0, :])
            new_m.append(st_m[:, :, :, 0, 0])

            w_uq = mla_w_uq[j].reshape(MLA_Q_RANK, MLA_HEADS, MLA_NOPE + MLA_ROPE)
            w_uq = jnp.concatenate([w_uq[:, :, :MLA_NOPE].reshape(MLA_Q_RANK, -1),
                                    w_uq[:, :, MLA_NOPE:].reshape(MLA_Q_RANK, -1)], axis=1).astype(BF16)
            cq = _nmm(proj, (4 * hw) // MLA_Q_RANK, MLA_Q_RANK, mla_q_norm_g[j], w_uq, 512)
            w_ukv = mla_w_ukv[j].astype(BF16)
            kv, ckv = _nmm(proj, (4 * hw + MLA_Q_RANK) // MLA_KV_RANK, MLA_KV_RANK, mla_kv_norm_g[j], w_ukv, 512,
                           with_aux=True)
            kv_cache = _mm(cache_mla_ckv[:, j].reshape(bs * past, MLA_KV_RANK), w_ukv, past, 512)
            new_ckv.append(ckv[:n_prompt].reshape(bp, tp, MLA_KV_RANK))
            kpe_col = 4 * hw + MLA_Q_RANK + MLA_KV_RANK
            new_kpe.append(proj[:n_prompt, kpe_col:kpe_col + MLA_ROPE].reshape(bp, tp, MLA_ROPE))

            mla_p = _mla_attn(cq, kv, proj, 0, bp, tp, tp)
            mla_s = _mla_attn(cq, kv, proj, n_prompt, bs, ts, tq_s,
                              cache=(kv_cache, cache_mla_kpe[:, j]), rope64=rope64)
            mix_a = _merge_rows(ml_p, ml_s)
            mix_b = _merge_rows(mla_p, mla_s)
            w_out = odd_w_out[j].astype(BF16)

        x = _proj_res(mix_a, mix_b, w_out, x, mods, rows, l, 2)
        x = _ffn(x, norm2_g[l], mods, rows, l, ffn_w_up[l].astype(BF16), ffn_conv_w[l], ffn_conv_b[l],
                 ffn_w_down[l].astype(BF16), keep_prev, keep_next)

    y = _final_norm(x, final_norm_g)
    y_prompt = y[:n_prompt].reshape(bp, tp, d)
    y_sample = y[n_prompt:].reshape(bs, ts, d)
    return (y_prompt, y_sample,
            jnp.stack(new_da_k, axis=1), jnp.stack(new_da_v, axis=1),
            jnp.stack(new_gq_k, axis=1), jnp.stack(new_gq_v, axis=1),
            jnp.stack(new_ckv, axis=1), jnp.stack(new_kpe, axis=1),
            jnp.stack(new_c, axis=1), jnp.stack(new_n, axis=1), jnp.stack(new_m, axis=1))
```

```python
import functools
import math

import jax
import jax.numpy as jnp
from jax import lax
from jax.experimental import pallas as pl
from jax.experimental.pallas import tpu as pltpu

F32 = jnp.float32
BF16 = jnp.bfloat16

EPS = 1e-6
GRID_W = 64
ROPE_THETA = 10000.0
HEAD_DIM = 128
DA_HEADS = 8
DA_SUB = HEAD_DIM // 2
GQ_HEADS = 8
GQ_KV_HEADS = 2
GQ_REP = GQ_HEADS // GQ_KV_HEADS
ML_HEADS = 8
ML_DK = 128
ML_DV = 128
MLA_HEADS = 8
MLA_Q_RANK = 512
MLA_KV_RANK = 256
MLA_NOPE = 128
MLA_ROPE = 64
MLA_V = 128

LANES = 128
MOD_ROWS = 8
VMEM_LIMIT = 56 * 1024 * 1024

ROW_TILE = 1024
ML_CHUNK = 256


def _cparams(sem):
    return pltpu.CompilerParams(dimension_semantics=sem, vmem_limit_bytes=VMEM_LIMIT)


def _dot(a, b):
    return jnp.dot(a, b, preferred_element_type=F32)


def _dot_nt(a, b):
    return lax.dot_general(a, b, (((1,), (1,)), ((), ())), preferred_element_type=F32)


def _rms_rows(x, g):
    return x * lax.rsqrt(jnp.mean(x * x, axis=-1, keepdims=True) + EPS) * g


def _sigmoid(x):
    return 1.0 / (1.0 + jnp.exp(-x))


def _softmax_rows(s):
    m = jnp.max(s, axis=-1, keepdims=True)
    p = jnp.exp(s - m)
    return p / jnp.sum(p, axis=-1, keepdims=True)


def _rope_lanes(x, cos, sin_a, sin_b, shift):
    up = pltpu.roll(x, LANES - shift, axis=1)
    dn = pltpu.roll(x, shift, axis=1)
    return x * cos + up * sin_a + dn * sin_b


def _rope_tables(n_tok, dim):
    t = jnp.arange(n_tok)
    r = (t // GRID_W).astype(F32)
    col = (t % GRID_W).astype(F32)
    quarter = dim // 4
    inv = ROPE_THETA ** (-jnp.arange(quarter, dtype=F32) / quarter)
    ang = jnp.concatenate([r[:, None] * inv, r[:, None] * inv, col[:, None] * inv, col[:, None] * inv], axis=1)
    ang = jnp.tile(ang, (1, LANES // dim))
    lane = jnp.arange(LANES)
    lower = ((lane // quarter) % 2 == 0)[None, :]
    cos, sin = jnp.cos(ang), jnp.sin(ang)
    return cos, jnp.where(lower, -sin, 0.0), jnp.where(lower, 0.0, sin)


def _adaln_kernel(ct_ref, w_ref, b_ref, o_ref, *, n_cond, k_chunk):
    d = w_ref.shape[0]
    tn = w_ref.shape[1]

    def body(kc, accs):
        k0 = pl.multiple_of(kc * k_chunk, k_chunk)
        w = w_ref[pl.ds(k0, k_chunk), :]
        c = ct_ref[pl.ds(k0, k_chunk), :]
        s = c * _sigmoid(c)
        return tuple(acc + jnp.sum(w * s[:, r:r + 1], axis=0, keepdims=True) for r, acc in enumerate(accs))

    accs = lax.fori_loop(0, d // k_chunk, body, tuple(jnp.zeros((1, tn), F32) for _ in range(n_cond)))
    o_ref[...] = jnp.zeros(o_ref.shape, F32)
    for r in range(n_cond):
        o_ref[r:r + 1, :] = accs[r] + b_ref[...]


def _adaln(cond_t, w_mod, b_mod, n_cond):
    depth, d, n6 = w_mod.shape
    tn = 1024
    return pl.pallas_call(
        functools.partial(_adaln_kernel, n_cond=n_cond, k_chunk=256),
        out_shape=jax.ShapeDtypeStruct((depth, MOD_ROWS, n6), F32),
        grid=(depth, n6 // tn),
        in_specs=[
            pl.BlockSpec((d, LANES), lambda l, j: (0, 0)),
            pl.BlockSpec((None, d, tn), lambda l, j: (l, 0, j)),
            pl.BlockSpec((None, 1, tn), lambda l, j: (l, 0, j)),
        ],
        out_specs=pl.BlockSpec((None, MOD_ROWS, tn), lambda l, j: (l, 0, j)),
        compiler_params=_cparams(("parallel", "parallel")),
        name="adaln",
    )(cond_t, w_mod, b_mod.reshape(depth, 1, n6))


class _Rows:
    def __init__(self, bp, tp, bs, ts, tm):
        assert (bp * tp) % tm == 0 and tm % tp == 0 and tm == ts
        self.n_prompt = bp * tp
        self.n_rows = bp * tp + bs * ts
        self.tm = tm
        self.prompt_tiles = self.n_prompt // tm
        self.tiles_per_sample = ts // tm

    def mod_row(self, i):
        return jnp.where(i < self.prompt_tiles, 0, 1 + (i - self.prompt_tiles) // self.tiles_per_sample)


def _norm_chunks(x_ref, h_ref, g_ref, sh_ref, sc_ref, aux_ref, rc):
    tm = x_ref.shape[0]

    def body(r, carry):
        r0 = pl.multiple_of(r * rc, rc)
        y = _rms_rows(x_ref[pl.ds(r0, rc), :], g_ref[...])
        if sc_ref is not None:
            y = y * (1.0 + sc_ref[0]) + sh_ref[0]
        if aux_ref is not None:
            aux_ref[pl.ds(r0, rc), :] = y
        h_ref[pl.ds(r0, rc), :] = y.astype(BF16)
        return carry

    lax.fori_loop(0, tm // rc, body, 0)


def _nmm_kernel(*refs, modulated, with_aux):
    it = iter(refs)
    x_ref, g_ref = next(it), next(it)
    sh_ref = sc_ref = None
    if modulated:
        sh_ref, sc_ref = next(it), next(it)
    w_ref, o_ref = next(it), next(it)
    aux_ref = next(it) if with_aux else None
    h_ref = next(it)

    @pl.when(pl.program_id(1) == 0)
    def _():
        _norm_chunks(x_ref, h_ref, g_ref, sh_ref, sc_ref, aux_ref, 128)

    o_ref[...] = _dot(h_ref[...], w_ref[...])


def _nmm(x, col_block, k, g, w, wl, tn, rows=None, mods=None, layer=0, chunks=(0, 1), with_aux=False, tm=ROW_TILE):
    n_rows = x.shape[0]
    n = w.shape[2]
    in_specs = [pl.BlockSpec((tm, k), lambda i, j: (i, col_block)),
                pl.BlockSpec((1, k), lambda i, j: (0, 0))]
    args = [x, g.reshape(1, k)]
    if mods is not None:
        for c in chunks:
            in_specs.append(pl.BlockSpec(
                (1, 1, k), lambda i, j, c=c: (layer * MOD_ROWS + rows.mod_row(i), 0, c)))
            args.append(mods)
    in_specs.append(pl.BlockSpec((None, k, tn), lambda i, j: (wl, 0, j)))
    args.append(w)
    out_shape = [jax.ShapeDtypeStruct((n_rows, n), F32)]
    out_specs = [pl.BlockSpec((tm, tn), lambda i, j: (i, j))]
    if with_aux:
        out_shape.append(jax.ShapeDtypeStruct((n_rows, k), F32))
        out_specs.append(pl.BlockSpec((tm, k), lambda i, j: (i, 0)))
    res = pl.pallas_call(
        functools.partial(_nmm_kernel, modulated=mods is not None, with_aux=with_aux),
        out_shape=out_shape,
        grid=(n_rows // tm, n // tn),
        in_specs=in_specs,
        out_specs=out_specs,
        scratch_shapes=[pltpu.VMEM((tm, k), BF16)],
        compiler_params=_cparams(("parallel", "arbitrary")),
        name="norm_matmul",
    )(*args)
    return res if with_aux else res[0]


def _mm_kernel(a_ref, w_ref, o_ref):
    o_ref[...] = _dot(a_ref[...].astype(BF16), w_ref[...])


def _mm(a, w, wl, tm, tn):
    m, k = a.shape
    n = w.shape[2]
    return pl.pallas_call(
        _mm_kernel,
        out_shape=jax.ShapeDtypeStruct((m, n), F32),
        grid=(m // tm, n // tn),
        in_specs=[pl.BlockSpec((tm, k), lambda i, j: (i, 0)),
                  pl.BlockSpec((None, k, tn), lambda i, j: (wl, 0, j))],
        out_specs=pl.BlockSpec((tm, tn), lambda i, j: (i, j)),
        compiler_params=_cparams(("parallel", "parallel")),
        name="matmul",
    )(a, w)


def _proj_res_kernel(a1p_ref, a2p_ref, a1s_ref, a2s_ref, w1_ref, w2_ref, x_ref, gate_ref, o_ref, *, prompt_tiles):
    def emit(a1_ref, a2_ref):
        acc = _dot(a1_ref[...], w1_ref[...]) + _dot(a2_ref[...], w2_ref[...])
        o_ref[...] = x_ref[...] + gate_ref[0] * acc

    @pl.when(pl.program_id(0) < prompt_tiles)
    def _():
        emit(a1p_ref, a2p_ref)

    @pl.when(pl.program_id(0) >= prompt_tiles)
    def _():
        emit(a1s_ref, a2s_ref)


def _proj_res(a1p, a2p, a1s, a2s, w, wl, x, mods, rows, layer, chunk, tn=512):
    tm = rows.tm
    n_rows = x.shape[0]
    k1 = a1p.shape[1]
    n = w.shape[2]
    per = n // tn
    pt = rows.prompt_tiles

    def prow(i):
        return jnp.minimum(i, pt - 1)

    def srow(i):
        return jnp.maximum(i - pt, 0)

    return pl.pallas_call(
        functools.partial(_proj_res_kernel, prompt_tiles=pt),
        out_shape=jax.ShapeDtypeStruct((n_rows, n), F32),
        grid=(n_rows // tm, n // tn),
        in_specs=[
            pl.BlockSpec((tm, k1), lambda i, j: (prow(i), 0)),
            pl.BlockSpec((tm, k1), lambda i, j: (prow(i), 0)),
            pl.BlockSpec((tm, k1), lambda i, j: (srow(i), 0)),
            pl.BlockSpec((tm, k1), lambda i, j: (srow(i), 0)),
            pl.BlockSpec((None, k1, tn), lambda i, j: (wl, 0, j)),
            pl.BlockSpec((None, k1, tn), lambda i, j: (wl, 1, j)),
            pl.BlockSpec((tm, tn), lambda i, j: (i, j)),
            pl.BlockSpec((1, 1, tn), lambda i, j: (layer * MOD_ROWS + rows.mod_row(i), 0, chunk * per + j)),
        ],
        out_specs=pl.BlockSpec((tm, tn), lambda i, j: (i, j)),
        compiler_params=_cparams(("parallel", "parallel")),
        name="proj_residual",
    )(a1p, a2p, a1s, a2s, w, w, x, mods)


def _ffn_kernel(x_ref, g_ref, sh_ref, sc_ref, gate_ref, kp_ref, kn_ref,
                wg_ref, wu_ref, cwg_ref, cwu_ref, cbg_ref, cbu_ref, wd_ref, o_ref,
                h_ref, act_ref, *, nj):
    j = pl.program_id(1)
    tm = x_ref.shape[0]

    def conv(w_ref, cw_ref, cb_ref):
        u = _dot(h_ref[...], w_ref[...])
        prev = jnp.where(kp_ref[...] > 0.0, pltpu.roll(u, 1, axis=0), 0.0)
        nxt = jnp.where(kn_ref[...] > 0.0, pltpu.roll(u, tm - 1, axis=0), 0.0)
        return cb_ref[...] + prev * cw_ref[0:1, :] + u * cw_ref[1:2, :] + nxt * cw_ref[2:3, :]

    def stage(slot):
        cg = conv(wg_ref, cwg_ref, cbg_ref)
        cu = conv(wu_ref, cwu_ref, cbu_ref)
        act_ref[slot] = (cg * _sigmoid(cg) * cu).astype(BF16)

    def contract(slot):
        return _dot(act_ref[slot], wd_ref[...])

    @pl.when(j == 0)
    def _():
        _norm_chunks(x_ref, h_ref, g_ref, sh_ref, sc_ref, None, 128)
        o_ref[...] = jnp.zeros(o_ref.shape, F32)
        stage(0)

    @pl.when(jnp.logical_and(j > 0, j < nj))
    def _():
        slot = j % 2
        stage(slot)
        o_ref[...] += contract(1 - slot)

    @pl.when(j == nj)
    def _():
        o_ref[...] = x_ref[...] + gate_ref[0] * (o_ref[...] + contract((nj - 1) % 2))


def _ffn(x, g, mods, rows, layer, w_up, conv_w, conv_b, w_down, keep_prev, keep_next, tn=512):
    n_rows, d = x.shape
    tm = rows.tm
    d_ff = w_down.shape[1]
    nj = d_ff // tn

    def mod_spec(c):
        return pl.BlockSpec((1, 1, d), lambda i, j: (layer * MOD_ROWS + rows.mod_row(i), 0, c))

    def up(j):
        return jnp.minimum(j, nj - 1)

    def down(j):
        return jnp.maximum(j - 1, 0)

    once = pl.Buffered(1)
    return pl.pallas_call(
        functools.partial(_ffn_kernel, nj=nj),
        out_shape=jax.ShapeDtypeStruct((n_rows, d), F32),
        grid=(n_rows // tm, nj + 1),
        in_specs=[
            pl.BlockSpec((tm, d), lambda i, j: (i, 0), pipeline_mode=once),
            pl.BlockSpec((1, d), lambda i, j: (0, 0)),
            mod_spec(3), mod_spec(4), mod_spec(5),
            pl.BlockSpec((tm, 1), lambda i, j: (i, 0)),
            pl.BlockSpec((tm, 1), lambda i, j: (i, 0)),
            pl.BlockSpec((None, d, tn), lambda i, j: (layer, 0, up(j))),
            pl.BlockSpec((None, d, tn), lambda i, j: (layer, 0, nj + up(j))),
            pl.BlockSpec((3, tn), lambda i, j: (0, up(j))),
            pl.BlockSpec((3, tn), lambda i, j: (0, nj + up(j))),
            pl.BlockSpec((1, tn), lambda i, j: (0, up(j))),
            pl.BlockSpec((1, tn), lambda i, j: (0, nj + up(j))),
            pl.BlockSpec((None, tn, d), lambda i, j: (layer, down(j), 0)),
        ],
        out_specs=pl.BlockSpec((tm, d), lambda i, j: (i, 0), pipeline_mode=once),
        scratch_shapes=[pltpu.VMEM((tm, d), BF16), pltpu.VMEM((2, tm, tn), BF16)],
        compiler_params=_cparams(("parallel", "arbitrary")),
        name="conv_ffn",
    )(x, g.reshape(1, d), mods, mods, mods, keep_prev, keep_next,
      w_up, w_up, conv_w, conv_w, conv_b.reshape(1, -1), conv_b.reshape(1, -1), w_down)


def _final_norm_kernel(x_ref, g_ref, o_ref):
    o_ref[...] = _rms_rows(x_ref[...], g_ref[...])


def _final_norm(x, g, row0, n_rows, tm=256):
    d = x.shape[1]
    return pl.pallas_call(
        _final_norm_kernel,
        out_shape=jax.ShapeDtypeStruct((n_rows, d), F32),
        grid=(n_rows // tm,),
        in_specs=[pl.BlockSpec((tm, d), lambda i: (row0 // tm + i, 0)), pl.BlockSpec((1, d), lambda i: (0, 0))],
        out_specs=pl.BlockSpec((tm, d), lambda i: (i, 0)),
        compiler_params=_cparams(("parallel",)),
        name="final_norm",
    )(x, g.reshape(1, d))


def _even_attn_kernel(*refs, past, use_rope, emit_cache, lam_init):
    it = iter(refs)
    qa_ref, qb_ref = next(it), next(it)
    ka_ref, va_ref, kb_ref, vb_ref = next(it), next(it), next(it), next(it)
    lam_ref, subln_ref, qn_ref, kn_ref = next(it), next(it), next(it), next(it)
    if past:
        cdak_ref, cdav_ref, cgqk_ref, cgqv_ref = next(it), next(it), next(it), next(it)
    if use_rope:
        q64_ref, k64_ref, q128_ref, k128_ref = next(it), next(it), next(it), next(it)
    oa_ref, ob_ref = next(it), next(it)
    if emit_cache:
        dak_ref, dav_ref, gqk_ref, gqv_ref = next(it), next(it), next(it), next(it)
    kda_s, vda_s, kgq_s, vgq_s = next(it), next(it), next(it), next(it)

    t_len = ka_ref.shape[0]

    @pl.when(pl.program_id(1) == 0)
    def _():
        for h in range(DA_HEADS):
            cols = slice(h * HEAD_DIM, (h + 1) * HEAD_DIM)
            k = ka_ref[:, cols]
            v = va_ref[:, cols]
            if emit_cache:
                dak_ref[h, 0] = k[:, :DA_SUB]
                dak_ref[h, 1] = k[:, DA_SUB:]
                dav_ref[h] = v
            if use_rope:
                k = _rope_lanes(k, k64_ref[0], k64_ref[1], k64_ref[2], DA_SUB // 4)
            kda_s[h, 0, past:, :] = k[:, :DA_SUB].astype(BF16)
            kda_s[h, 1, past:, :] = k[:, DA_SUB:].astype(BF16)
            vda_s[h, past:, :] = v.astype(BF16)
            if past:
                kda_s[h, 0, :past, :] = cdak_ref[h, 0].astype(BF16)
                kda_s[h, 1, :past, :] = cdak_ref[h, 1].astype(BF16)
                vda_s[h, :past, :] = cdav_ref[h].astype(BF16)
        for g in range(GQ_KV_HEADS):
            cols = slice(g * HEAD_DIM, (g + 1) * HEAD_DIM)
            k = _rms_rows(kb_ref[:, cols], kn_ref[...])
            v = vb_ref[:, cols]
            if emit_cache:
                gqk_ref[g] = k
                gqv_ref[g] = v
            if use_rope:
                k = _rope_lanes(k, k128_ref[0], k128_ref[1], k128_ref[2], HEAD_DIM // 4)
            kgq_s[g, past:, :] = k.astype(BF16)
            vgq_s[g, past:, :] = v.astype(BF16)
            if past:
                kgq_s[g, :past, :] = cgqk_ref[g].astype(BF16)
                vgq_s[g, :past, :] = cgqv_ref[g].astype(BF16)

    lam4 = lam_ref[...]
    lam = (jnp.exp(jnp.sum(lam4[0:1] * lam4[1:2], axis=-1, keepdims=True))
           - jnp.exp(jnp.sum(lam4[2:3] * lam4[3:4], axis=-1, keepdims=True)) + lam_init)

    da_scale = DA_SUB ** -0.5
    for h in range(DA_HEADS):
        cols = slice(h * HEAD_DIM, (h + 1) * HEAD_DIM)
        q = qa_ref[:, cols]
        if use_rope:
            q = _rope_lanes(q, q64_ref[0], q64_ref[1], q64_ref[2], DA_SUB // 4)
        a1 = _softmax_rows(_dot_nt(q[:, :DA_SUB].astype(BF16), kda_s[h, 0]) * da_scale)
        a2 = _softmax_rows(_dot_nt(q[:, DA_SUB:].astype(BF16), kda_s[h, 1]) * da_scale)
        o = _dot((a1 - lam * a2).astype(BF16), vda_s[h])
        oa_ref[:, cols] = (_rms_rows(o, subln_ref[...]) * (1.0 - lam_init)).astype(BF16)

    gq_scale = HEAD_DIM ** -0.5
    for h in range(GQ_HEADS):
        cols = slice(h * HEAD_DIM, (h + 1) * HEAD_DIM)
        q = _rms_rows(qb_ref[:, cols], qn_ref[...])
        if use_rope:
            q = _rope_lanes(q, q128_ref[0], q128_ref[1], q128_ref[2], HEAD_DIM // 4)
        a = _softmax_rows(_dot_nt(q.astype(BF16), kgq_s[h // GQ_REP]) * gq_scale)
        ob_ref[:, cols] = _dot(a.astype(BF16), vgq_s[h // GQ_REP]).astype(BF16)


def _even_attn(proj, row0, nb, t_len, tq, lam, subln_g, qn_g, kn_g, lam_init,
               caches=None, ropes=None, emit_cache=False):
    n_rows = proj.shape[0]
    hd = DA_HEADS * HEAD_DIM
    kvw = GQ_KV_HEADS * HEAD_DIM
    nq = t_len // tq
    qb0 = row0 // tq
    kb0 = row0 // t_len
    past = caches[0].shape[3] if caches is not None else 0
    s_len = past + t_len

    def qspec(cb):
        return pl.BlockSpec((tq, hd), lambda b, q: (qb0 + b * nq + q, cb))

    def kspec(width, cb):
        return pl.BlockSpec((t_len, width), lambda b, q: (kb0 + b, cb))

    def const(shape):
        return pl.BlockSpec(shape, lambda b, q: (0,) * len(shape))

    in_specs = [qspec(0), qspec(3), kspec(hd, 1), kspec(hd, 2), kspec(kvw, 4 * hd // kvw), kspec(kvw, 4 * hd // kvw + 1),
                const((4, DA_SUB)), const((1, HEAD_DIM)), const((1, HEAD_DIM)), const((1, HEAD_DIM))]
    args = [proj, proj, proj, proj, proj, proj, lam, subln_g.reshape(1, -1), qn_g.reshape(1, -1), kn_g.reshape(1, -1)]
    if caches is not None:
        cdak, cdav, cgqk, cgqv = caches
        in_specs += [
            pl.BlockSpec((None, DA_HEADS, 2, past, DA_SUB), lambda b, q: (b, 0, 0, 0, 0)),
            pl.BlockSpec((None, DA_HEADS, past, HEAD_DIM), lambda b, q: (b, 0, 0, 0)),
            pl.BlockSpec((None, GQ_KV_HEADS, past, HEAD_DIM), lambda b, q: (b, 0, 0, 0)),
            pl.BlockSpec((None, GQ_KV_HEADS, past, HEAD_DIM), lambda b, q: (b, 0, 0, 0)),
        ]
        args += [cdak, cdav, cgqk, cgqv]
    if ropes is not None:
        r64, r128 = ropes
        in_specs += [pl.BlockSpec((3, tq, LANES), lambda b, q: (0, q, 0)),
                     pl.BlockSpec((3, t_len, LANES), lambda b, q: (0, 0, 0)),
                     pl.BlockSpec((3, tq, LANES), lambda b, q: (0, q, 0)),
                     pl.BlockSpec((3, t_len, LANES), lambda b, q: (0, 0, 0))]
        args += [r64, r64, r128, r128]
    out_shape = [jax.ShapeDtypeStruct((nb * t_len, hd), BF16), jax.ShapeDtypeStruct((nb * t_len, hd), BF16)]
    out_specs = [pl.BlockSpec((tq, hd), lambda b, q: (b * nq + q, 0)),
                 pl.BlockSpec((tq, hd), lambda b, q: (b * nq + q, 0))]
    if emit_cache:
        out_shape += [jax.ShapeDtypeStruct((nb, DA_HEADS, 2, t_len, DA_SUB), F32),
                      jax.ShapeDtypeStruct((nb, DA_HEADS, t_len, HEAD_DIM), F32),
                      jax.ShapeDtypeStruct((nb, GQ_KV_HEADS, t_len, HEAD_DIM), F32),
                      jax.ShapeDtypeStruct((nb, GQ_KV_HEADS, t_len, HEAD_DIM), F32)]
        out_specs += [pl.BlockSpec((None, DA_HEADS, 2, t_len, DA_SUB), lambda b, q: (b, 0, 0, 0, 0)),
                      pl.BlockSpec((None, DA_HEADS, t_len, HEAD_DIM), lambda b, q: (b, 0, 0, 0)),
                      pl.BlockSpec((None, GQ_KV_HEADS, t_len, HEAD_DIM), lambda b, q: (b, 0, 0, 0)),
                      pl.BlockSpec((None, GQ_KV_HEADS, t_len, HEAD_DIM), lambda b, q: (b, 0, 0, 0))]
    return pl.pallas_call(
        functools.partial(_even_attn_kernel, past=past, use_rope=ropes is not None,
                          emit_cache=emit_cache, lam_init=lam_init),
        out_shape=out_shape,
        grid=(nb, nq),
        in_specs=in_specs,
        out_specs=out_specs,
        scratch_shapes=[pltpu.VMEM((DA_HEADS, 2, s_len, DA_SUB), BF16),
                        pltpu.VMEM((DA_HEADS, s_len, HEAD_DIM), BF16),
                        pltpu.VMEM((GQ_KV_HEADS, s_len, HEAD_DIM), BF16),
                        pltpu.VMEM((GQ_KV_HEADS, s_len, HEAD_DIM), BF16)],
        compiler_params=_cparams(("parallel", "arbitrary")),
        name="even_attention",
    )(*args)


def _mlstm_chunk(qc, kc, vc, lf_r, lf_c, li_r, li_c, state, reverse):
    n = qc.shape[0]
    ti = lax.broadcasted_iota(jnp.int32, (n, n), 0)
    si = lax.broadcasted_iota(jnp.int32, (n, n), 1)
    seen = (si >= ti) if reverse else (si <= ti)
    b_c = jnp.sum(jnp.where(seen, lf_r, 0.0), axis=1, keepdims=True)
    seen_t = (ti >= si) if reverse else (ti <= si)
    b_r = jnp.sum(jnp.where(seen_t, lf_c, 0.0), axis=0, keepdims=True)
    b_end = b_r[:, 0:1] if reverse else b_r[:, n - 1:n]
    d_intra = jnp.where(seen, b_c - b_r + li_r, -jnp.inf)
    row_max = jnp.max(d_intra, axis=1, keepdims=True)
    qb, kb, vb = qc.astype(BF16), kc.astype(BF16), vc.astype(BF16)
    if state is None:
        m_prev = jnp.zeros((1, 1), F32)
    else:
        c_prev, n_prev, m_prev = state
    d_inter = b_c + m_prev
    m_t = jnp.maximum(d_inter, row_max)
    s = _dot_nt(qb, kb) * jnp.exp(d_intra - m_t)
    num = _dot(s.astype(BF16), vb)
    den = jnp.sum(s, axis=1, keepdims=True)
    if state is not None:
        a = jnp.exp(d_inter - m_t)
        num = num + a * _dot(qb, c_prev.astype(BF16))
        den = den + a * jnp.sum(qc * n_prev, axis=1, keepdims=True)
    h = num / jnp.maximum(jnp.abs(den), jnp.exp(-m_t))
    g_c = b_end - b_c + li_c
    g_r = b_end - b_r + li_r
    m_new = jnp.maximum(b_end + m_prev, jnp.max(g_r, axis=1, keepdims=True))
    kw = kc * jnp.exp(g_c - m_new)
    c_new = _dot(kw.T.astype(BF16), vb)
    n_new = jnp.sum(kw, axis=0, keepdims=True)
    if state is not None:
        decay = jnp.exp(b_end + m_prev - m_new)
        c_new = decay * c_prev + c_new
        n_new = decay * n_prev + n_new
    return h, (c_new, n_new, m_new)


def _mlstm_kernel(*refs, has_state, emit_state, chunk):
    it = iter(refs)
    q_ref, k_ref, v_ref, o_ref, gt_ref = next(it), next(it), next(it), next(it), next(it)
    cwq_ref, cwk_ref, cbq_ref, cbk_ref, gb_ref, ng_ref = next(it), next(it), next(it), next(it), next(it), next(it)
    if has_state:
        c0_ref, n0_ref, m0_ref = next(it), next(it), next(it)
    mix_ref = next(it)
    if emit_state:
        co_ref, no_ref, mo_ref = next(it), next(it), next(it)
    hacc = next(it)

    t_len = q_ref.shape[0]
    head = pl.program_id(1)
    row = lax.broadcasted_iota(jnp.int32, (t_len, 1), 0)

    def conv_silu(x_ref, w_ref, b_ref):
        x = x_ref[...]
        prev = jnp.where(row == 0, 0.0, pltpu.roll(x, 1, axis=0))
        nxt = jnp.where(row == t_len - 1, 0.0, pltpu.roll(x, t_len - 1, axis=0))
        y = b_ref[...] + prev * w_ref[0:1, :] + x * w_ref[1:2, :] + nxt * w_ref[2:3, :]
        return y * _sigmoid(y)

    q = conv_silu(q_ref, cwq_ref, cbq_ref) * (ML_DK ** -0.5)
    k = conv_silu(k_ref, cwk_ref, cbk_ref)
    v = v_ref[...]

    gates = gt_ref[...] + gb_ref[...]
    gates_t = gates.T
    lane = lax.broadcasted_iota(jnp.int32, gates.shape, 1)
    sub = lax.broadcasted_iota(jnp.int32, gates_t.shape, 0)

    def gate_col(kind):
        return jnp.sum(jnp.where(lane == MLA_ROPE + kind * ML_HEADS + head, gates, 0.0), axis=1, keepdims=True)

    def gate_row(kind):
        return jnp.sum(jnp.where(sub == MLA_ROPE + kind * ML_HEADS + head, gates_t, 0.0), axis=0, keepdims=True)

    def log_sigmoid(x):
        return jnp.minimum(x, 0.0) - jnp.log(1.0 + jnp.exp(-jnp.abs(x)))

    n_chunks = t_len // chunk
    for d in range(2):
        li_c, li_r = gate_col(2 * d), gate_row(2 * d)
        lf_c, lf_r = log_sigmoid(gate_col(2 * d + 1)), log_sigmoid(gate_row(2 * d + 1))
        state = (c0_ref[d], n0_ref[d], m0_ref[d][:, 0:1]) if has_state else None
        order = range(n_chunks) if d == 0 else range(n_chunks - 1, -1, -1)
        for c in order:
            sl = slice(c * chunk, (c + 1) * chunk)
            h, state = _mlstm_chunk(q[sl], k[sl], v[sl], lf_r[:, sl], lf_c[sl], li_r[:, sl], li_c[sl],
                                    state, reverse=(d == 1))
            if d == 0:
                hacc[sl, :] = h
            else:
                hacc[sl, :] += h
        if emit_state:
            co_ref[d] = state[0]
            no_ref[d] = state[1]
            mo_ref[d] = jnp.broadcast_to(state[2], (1, LANES))

    hn = _rms_rows(hacc[...], ng_ref[...])
    mix_ref[...] = (hn * _sigmoid(o_ref[...])).astype(BF16)


def _mlstm(proj, row0, nb, t_len, conv_w, conv_b, gate_b_lanes, norm_g, states=None, emit_state=False):
    n_rows = proj.shape[0]
    hw = ML_HEADS * ML_DK
    kb0 = row0 // t_len
    nh = ML_HEADS
    chunk = min(ML_CHUNK, t_len)
    assert t_len % chunk == 0
    gate_block = (4 * hw + MLA_Q_RANK + MLA_KV_RANK) // LANES

    def tok(cb):
        return pl.BlockSpec((t_len, ML_DK), lambda b, h: (kb0 + b, cb * nh + h))

    in_specs = [tok(0), tok(1), tok(2), tok(3),
                pl.BlockSpec((t_len, LANES), lambda b, h: (kb0 + b, gate_block)),
                pl.BlockSpec((3, ML_DK), lambda b, h: (0, h)),
                pl.BlockSpec((3, ML_DK), lambda b, h: (0, nh + h)),
                pl.BlockSpec((1, ML_DK), lambda b, h: (0, h)),
                pl.BlockSpec((1, ML_DK), lambda b, h: (0, nh + h)),
                pl.BlockSpec((1, LANES), lambda b, h: (0, 0)),
                pl.BlockSpec((1, ML_DV), lambda b, h: (0, 0))]
    args = [proj, proj, proj, proj, proj, conv_w, conv_w, conv_b.reshape(1, -1), conv_b.reshape(1, -1),
            gate_b_lanes, norm_g.reshape(1, -1)]
    if states is not None:
        c0, n0, m0 = states
        in_specs += [pl.BlockSpec((None, 2, None, ML_DK, ML_DV), lambda b, h: (b, 0, h, 0, 0)),
                     pl.BlockSpec((None, 2, None, 1, ML_DK), lambda b, h: (b, 0, h, 0, 0)),
                     pl.BlockSpec((None, 2, None, 1, LANES), lambda b, h: (b, 0, h, 0, 0))]
        args += [c0, n0, m0]
    out_shape = [jax.ShapeDtypeStruct((nb * t_len, hw), BF16)]
    out_specs = [pl.BlockSpec((t_len, ML_DV), lambda b, h: (b, h))]
    if emit_state:
        out_shape += [jax.ShapeDtypeStruct((nb, 2, nh, ML_DK, ML_DV), F32),
                      jax.ShapeDtypeStruct((nb, 2, nh, 1, ML_DK), F32),
                      jax.ShapeDtypeStruct((nb, 2, nh, 1, LANES), F32)]
        out_specs += [pl.BlockSpec((None, 2, None, ML_DK, ML_DV), lambda b, h: (b, 0, h, 0, 0)),
                      pl.BlockSpec((None, 2, None, 1, ML_DK), lambda b, h: (b, 0, h, 0, 0)),
                      pl.BlockSpec((None, 2, None, 1, LANES), lambda b, h: (b, 0, h, 0, 0))]
    return pl.pallas_call(
        functools.partial(_mlstm_kernel, has_state=states is not None, emit_state=emit_state, chunk=chunk),
        out_shape=out_shape,
        grid=(nb, nh),
        in_specs=in_specs,
        out_specs=out_specs,
        scratch_shapes=[pltpu.VMEM((t_len, ML_DV), F32)],
        compiler_params=_cparams(("parallel", "parallel")),
        name="mlstm",
    )(*args)


def _mla_attn_kernel(*refs, past, use_rope):
    it = iter(refs)
    cq_ref, kv_ref, kpe_ref = next(it), next(it), next(it)
    if past:
        kvc_ref, kpec_ref = next(it), next(it)
    if use_rope:
        q64_ref, k64_ref = next(it), next(it)
    o_ref = next(it)
    kn_s, v_s, kpe_s = next(it), next(it), next(it)

    nope_w = MLA_HEADS * MLA_NOPE

    @pl.when(pl.program_id(1) == 0)
    def _():
        for h in range(MLA_HEADS):
            c0 = h * (MLA_NOPE + MLA_V)
            kn_s[h, past:, :] = kv_ref[:, c0:c0 + MLA_NOPE].astype(BF16)
            v_s[h, past:, :] = kv_ref[:, c0 + MLA_NOPE:c0 + MLA_NOPE + MLA_V].astype(BF16)
            if past:
                kn_s[h, :past, :] = kvc_ref[:, c0:c0 + MLA_NOPE].astype(BF16)
                v_s[h, :past, :] = kvc_ref[:, c0 + MLA_NOPE:c0 + MLA_NOPE + MLA_V].astype(BF16)
        kpe = kpe_ref[...]
        if use_rope:
            kpe = _rope_lanes(kpe, k64_ref[0], k64_ref[1], k64_ref[2], MLA_ROPE // 4)
        kpe_s[past:, :] = kpe[:, :MLA_ROPE].astype(BF16)
        if past:
            kpe_s[:past, :] = kpec_ref[...].astype(BF16)

    scale = (MLA_NOPE + MLA_ROPE) ** -0.5
    for h2 in range(MLA_HEADS // 2):
        qp2 = cq_ref[:, nope_w + h2 * LANES:nope_w + (h2 + 1) * LANES]
        if use_rope:
            qp2 = _rope_lanes(qp2, q64_ref[0], q64_ref[1], q64_ref[2], MLA_ROPE // 4)
        for e in range(2):
            h = 2 * h2 + e
            qn = cq_ref[:, h * MLA_NOPE:(h + 1) * MLA_NOPE].astype(BF16)
            qp = qp2[:, e * MLA_ROPE:(e + 1) * MLA_ROPE].astype(BF16)
            s = (_dot_nt(qn, kn_s[h]) + _dot_nt(qp, kpe_s[...])) * scale
            a = _softmax_rows(s)
            o_ref[:, h * MLA_V:(h + 1) * MLA_V] = _dot(a.astype(BF16), v_s[h]).astype(BF16)


def _mla_attn(cq, kv, proj, row0, nb, t_len, tq, cache=None, rope64=None):
    n_rows = cq.shape[0]
    nq = t_len // tq
    qb0 = row0 // tq
    kb0 = row0 // t_len
    past = cache[0].shape[0] // nb if cache is not None else 0
    s_len = past + t_len
    kpe_block = (4 * ML_HEADS * ML_DK + MLA_Q_RANK + MLA_KV_RANK) // LANES
    in_specs = [pl.BlockSpec((tq, cq.shape[1]), lambda b, q: (qb0 + b * nq + q, 0)),
                pl.BlockSpec((t_len, kv.shape[1]), lambda b, q: (kb0 + b, 0)),
                pl.BlockSpec((t_len, LANES), lambda b, q: (kb0 + b, kpe_block))]
    args = [cq, kv, proj]
    if cache is not None:
        kvc, kpec = cache
        in_specs += [pl.BlockSpec((past, kvc.shape[1]), lambda b, q: (b, 0)),
                     pl.BlockSpec((None, past, MLA_ROPE), lambda b, q: (b, 0, 0))]
        args += [kvc, kpec]
    if rope64 is not None:
        in_specs += [pl.BlockSpec((3, tq, LANES), lambda b, q: (0, q, 0)),
                     pl.BlockSpec((3, t_len, LANES), lambda b, q: (0, 0, 0))]
        args += [rope64, rope64]
    return pl.pallas_call(
        functools.partial(_mla_attn_kernel, past=past, use_rope=rope64 is not None),
        out_shape=jax.ShapeDtypeStruct((nb * t_len, MLA_HEADS * MLA_V), BF16),
        grid=(nb, nq),
        in_specs=in_specs,
        out_specs=pl.BlockSpec((tq, MLA_HEADS * MLA_V), lambda b, q: (b * nq + q, 0)),
        scratch_shapes=[pltpu.VMEM((MLA_HEADS, s_len, MLA_NOPE), BF16),
                        pltpu.VMEM((MLA_HEADS, s_len, MLA_V), BF16),
                        pltpu.VMEM((s_len, MLA_ROPE), BF16)],
        compiler_params=_cparams(("parallel", "arbitrary")),
        name="mla_attention",
    )(*args)


def kernel(x_prompt, x_sample, cache_da_k, cache_da_v, cache_gq_k, cache_gq_v, cache_mla_ckv, cache_mla_kpe, state_ml_C, state_ml_n, state_ml_m, c, c_ctx, norm1_g, norm2_g, w_mod, b_mod, even_w_in, even_w_out, da_lambda, da_subln_g, gq_q_norm_g, gq_k_norm_g, odd_w_in, odd_w_out, ml_conv_w, ml_conv_b, ml_gate_b, ml_norm_g, mla_q_norm_g, mla_w_uq, mla_kv_norm_g, mla_w_ukv, ffn_w_up, ffn_conv_w, ffn_conv_b, ffn_w_down, final_norm_g):
    bp, tp, d = x_prompt.shape
    bs, ts, _ = x_sample.shape
    depth = norm1_g.shape[0]
    past = cache_da_k.shape[4]
    n_prompt = bp * tp
    rows = _Rows(bp, tp, bs, ts, ROW_TILE)
    n_cond = 1 + bs
    assert n_cond <= MOD_ROWS

    x = jnp.concatenate([x_prompt.reshape(n_prompt, d), x_sample.reshape(bs * ts, d)], axis=0)

    cond_t = jnp.zeros((d, LANES), F32).at[:, :n_cond].set(jnp.concatenate([c_ctx[None, :], c], axis=0).T)
    mods = _adaln(cond_t, w_mod, b_mod, n_cond).reshape(depth * MOD_ROWS, 1, 6 * d)

    pos = jnp.concatenate([jnp.tile(jnp.arange(tp), bp), jnp.tile(jnp.arange(ts), bs)])
    seq_len = jnp.concatenate([jnp.full((n_prompt,), tp), jnp.full((bs * ts,), ts)])
    keep_prev = (pos != 0).astype(F32)[:, None]
    keep_next = (pos != seq_len - 1).astype(F32)[:, None]

    rope64 = jnp.stack(_rope_tables(ts, DA_SUB))
    rope128 = jnp.stack(_rope_tables(ts, HEAD_DIM))

    hw = ML_HEADS * ML_DK
    o_gate = 4 * hw
    o_qd = o_gate + 4 * ML_HEADS
    o_kvd = o_qd + MLA_Q_RANK
    o_kpe = o_kvd + MLA_KV_RANK
    n_in = odd_w_in.shape[2]
    n_pad = -(-n_in // 512) * 512
    odd_w_in_b = jnp.concatenate(
        [odd_w_in[:, :, :o_gate], odd_w_in[:, :, o_qd:o_kvd], odd_w_in[:, :, o_kvd:o_kpe], odd_w_in[:, :, o_kpe:],
         odd_w_in[:, :, o_gate:o_qd], jnp.zeros(odd_w_in.shape[:2] + (n_pad - n_in,), F32)], axis=2).astype(BF16)
    w_uq = mla_w_uq.reshape(-1, MLA_Q_RANK, MLA_HEADS, MLA_NOPE + MLA_ROPE)
    w_uq_b = jnp.concatenate([w_uq[..., :MLA_NOPE].reshape(-1, MLA_Q_RANK, MLA_HEADS * MLA_NOPE),
                              w_uq[..., MLA_NOPE:].reshape(-1, MLA_Q_RANK, MLA_HEADS * MLA_ROPE)], axis=2).astype(BF16)
    even_w_in_b, even_w_out_b = even_w_in.astype(BF16), even_w_out.astype(BF16)
    odd_w_out_b, w_ukv_b = odd_w_out.astype(BF16), mla_w_ukv.astype(BF16)
    ffn_w_up_b, ffn_w_down_b = ffn_w_up.astype(BF16), ffn_w_down.astype(BF16)

    tq_s = min(256, ts)
    new_da_k, new_da_v, new_gq_k, new_gq_v = [], [], [], []
    new_ckv, new_kpe, new_c, new_n, new_m = [], [], [], [], []

    for l in range(depth):
        j = l // 2
        if l % 2 == 0:
            lam_init = 0.8 - 0.6 * math.exp(-0.3 * l)
            proj = _nmm(x, 0, d, norm1_g[l], even_w_in_b, j, 512, rows=rows, mods=mods, layer=l)
            common = (da_lambda[j], da_subln_g[j], gq_q_norm_g[j], gq_k_norm_g[j], lam_init)
            a1p, a2p, dak, dav, gqk, gqv = _even_attn(proj, 0, bp, tp, tp, *common, emit_cache=True)
            a1s, a2s = _even_attn(proj, n_prompt, bs, ts, tq_s, *common,
                                  caches=(cache_da_k[:, j], cache_da_v[:, j], cache_gq_k[:, j], cache_gq_v[:, j]),
                                  ropes=(rope64, rope128))
            new_da_k.append(dak)
            new_da_v.append(dav)
            new_gq_k.append(gqk)
            new_gq_v.append(gqv)
            w_out = even_w_out_b
        else:
            proj = _nmm(x, 0, d, norm1_g[l], odd_w_in_b, j, 512, rows=rows, mods=mods, layer=l)
            gate_b_lanes = jnp.zeros((1, LANES), F32).at[0, MLA_ROPE:MLA_ROPE + 4 * ML_HEADS].set(ml_gate_b[j])

            a1p, st_c, st_n, st_m = _mlstm(proj, 0, bp, tp, ml_conv_w[j], ml_conv_b[j], gate_b_lanes,
                                           ml_norm_g[j], emit_state=True)
            m0 = jnp.broadcast_to(state_ml_m[:, j][..., None, None], (bs, 2, ML_HEADS, 1, LANES))
            (a1s,) = _mlstm(proj, n_prompt, bs, ts, ml_conv_w[j], ml_conv_b[j], gate_b_lanes, ml_norm_g[j],
                            states=(state_ml_C[:, j], state_ml_n[:, j][:, :, :, None, :], m0))
            new_c.append(st_c)
            new_n.append(st_n[:, :, :, 0, :])
            new_m.append(st_m[:, :, :, 0, 0])

            cq = _nmm(proj, (4 * hw) // MLA_Q_RANK, MLA_Q_RANK, mla_q_norm_g[j], w_uq_b, j, 512)
            kv, ckv = _nmm(proj, (4 * hw + MLA_Q_RANK) // MLA_KV_RANK, MLA_KV_RANK, mla_kv_norm_g[j], w_ukv_b, j, 512,
                           with_aux=True)
            kv_cache = _mm(cache_mla_ckv[:, j].reshape(bs * past, MLA_KV_RANK), w_ukv_b, j, past, 512)
            new_ckv.append(ckv[:n_prompt].reshape(bp, tp, MLA_KV_RANK))
            kpe_col = 4 * hw + MLA_Q_RANK + MLA_KV_RANK
            new_kpe.append(proj[:n_prompt, kpe_col:kpe_col + MLA_ROPE].reshape(bp, tp, MLA_ROPE))

            a2p = _mla_attn(cq, kv, proj, 0, bp, tp, tp)
            a2s = _mla_attn(cq, kv, proj, n_prompt, bs, ts, tq_s,
                            cache=(kv_cache, cache_mla_kpe[:, j]), rope64=rope64)
            w_out = odd_w_out_b

        x = _proj_res(a1p, a2p, a1s, a2s, w_out, j, x, mods, rows, l, 2)
        x = _ffn(x, norm2_g[l], mods, rows, l, ffn_w_up_b, ffn_conv_w[l], ffn_conv_b[l],
                 ffn_w_down_b, keep_prev, keep_next)

    y_prompt = _final_norm(x, final_norm_g, 0, n_prompt).reshape(bp, tp, d)
    y_sample = _final_norm(x, final_norm_g, n_prompt, bs * ts).reshape(bs, ts, d)
    return (y_prompt, y_sample,
            jnp.stack(new_da_k, axis=1), jnp.stack(new_da_v, axis=1),
            jnp.stack(new_gq_k, axis=1), jnp.stack(new_gq_v, axis=1),
            jnp.stack(new_ckv, axis=1), jnp.stack(new_kpe, axis=1),
            jnp.stack(new_c, axis=1), jnp.stack(new_n, axis=1), jnp.stack(new_m, axis=1))
```

```python
import functools
import math

import jax
import jax.numpy as jnp
from jax import lax
from jax.experimental import pallas as pl
from jax.experimental.pallas import tpu as pltpu

F32 = jnp.float32
BF16 = jnp.bfloat16

EPS = 1e-6
GRID_W = 64
ROPE_THETA = 10000.0
HEAD_DIM = 128
DA_HEADS = 8
DA_SUB = HEAD_DIM // 2
GQ_HEADS = 8
GQ_KV_HEADS = 2
GQ_REP = GQ_HEADS // GQ_KV_HEADS
ML_HEADS = 8
ML_DK = 128
ML_DV = 128
MLA_HEADS = 8
MLA_Q_RANK = 512
MLA_KV_RANK = 256
MLA_NOPE = 128
MLA_ROPE = 64
MLA_V = 128

LANES = 128
MOD_ROWS = 8
VMEM_LIMIT = 56 * 1024 * 1024

ROW_TILE = 1024
ML_CHUNK = 256


def _cparams(sem):
    return pltpu.CompilerParams(dimension_semantics=sem, vmem_limit_bytes=VMEM_LIMIT)


def _dot(a, b):
    return jnp.dot(a, b, preferred_element_type=F32)


def _dot_nt(a, b):
    return lax.dot_general(a, b, (((1,), (1,)), ((), ())), preferred_element_type=F32)


def _rms_rows(x, g):
    return x * lax.rsqrt(jnp.mean(x * x, axis=-1, keepdims=True) + EPS) * g


def _sigmoid(x):
    return 1.0 / (1.0 + jnp.exp(-x))


def _softmax_rows(s):
    m = jnp.max(s, axis=-1, keepdims=True)
    p = jnp.exp(s - m)
    return p / jnp.sum(p, axis=-1, keepdims=True)


def _rope_lanes(x, cos, sin_a, sin_b, shift):
    up = pltpu.roll(x, LANES - shift, axis=1)
    dn = pltpu.roll(x, shift, axis=1)
    return x * cos + up * sin_a + dn * sin_b


def _rope_tables(n_tok, dim):
    t = jnp.arange(n_tok)
    r = (t // GRID_W).astype(F32)
    col = (t % GRID_W).astype(F32)
    quarter = dim // 4
    inv = ROPE_THETA ** (-jnp.arange(quarter, dtype=F32) / quarter)
    ang = jnp.concatenate([r[:, None] * inv, r[:, None] * inv, col[:, None] * inv, col[:, None] * inv], axis=1)
    ang = jnp.tile(ang, (1, LANES // dim))
    lane = jnp.arange(LANES)
    lower = ((lane // quarter) % 2 == 0)[None, :]
    cos, sin = jnp.cos(ang), jnp.sin(ang)
    return cos, jnp.where(lower, -sin, 0.0), jnp.where(lower, 0.0, sin)


def _adaln_kernel(ct_ref, w_ref, b_ref, o_ref, *, n_cond, k_chunk):
    d = w_ref.shape[0]
    tn = w_ref.shape[1]

    def body(kc, accs):
        k0 = pl.multiple_of(kc * k_chunk, k_chunk)
        w = w_ref[pl.ds(k0, k_chunk), :]
        c = ct_ref[pl.ds(k0, k_chunk), :]
        s = c * _sigmoid(c)
        return tuple(acc + jnp.sum(w * s[:, r:r + 1], axis=0, keepdims=True) for r, acc in enumerate(accs))

    accs = lax.fori_loop(0, d // k_chunk, body, tuple(jnp.zeros((1, tn), F32) for _ in range(n_cond)))
    o_ref[...] = jnp.zeros(o_ref.shape, F32)
    for r in range(n_cond):
        o_ref[r:r + 1, :] = accs[r] + b_ref[...]


def _adaln(cond_t, w_mod, b_mod, n_cond):
    depth, d, n6 = w_mod.shape
    tn = 1024
    return pl.pallas_call(
        functools.partial(_adaln_kernel, n_cond=n_cond, k_chunk=256),
        out_shape=jax.ShapeDtypeStruct((depth, MOD_ROWS, n6), F32),
        grid=(depth, n6 // tn),
        in_specs=[
            pl.BlockSpec((d, LANES), lambda l, j: (0, 0)),
            pl.BlockSpec((None, d, tn), lambda l, j: (l, 0, j)),
            pl.BlockSpec((None, 1, tn), lambda l, j: (l, 0, j)),
        ],
        out_specs=pl.BlockSpec((None, MOD_ROWS, tn), lambda l, j: (l, 0, j)),
        compiler_params=_cparams(("parallel", "parallel")),
        name="adaln",
    )(cond_t, w_mod, b_mod.reshape(depth, 1, n6))


class _Rows:
    def __init__(self, bp, tp, bs, ts, tm):
        assert (bp * tp) % tm == 0 and tm % tp == 0 and tm == ts
        self.n_prompt = bp * tp
        self.n_rows = bp * tp + bs * ts
        self.tm = tm
        self.prompt_tiles = self.n_prompt // tm
        self.tiles_per_sample = ts // tm

    def mod_row(self, i):
        return jnp.where(i < self.prompt_tiles, 0, 1 + (i - self.prompt_tiles) // self.tiles_per_sample)


def _norm_chunks(x_ref, h_ref, g_ref, sh_ref, sc_ref, aux_ref, rc):
    tm = x_ref.shape[0]

    def body(r, carry):
        r0 = pl.multiple_of(r * rc, rc)
        y = _rms_rows(x_ref[pl.ds(r0, rc), :], g_ref[...])
        if sc_ref is not None:
            y = y * (1.0 + sc_ref[0]) + sh_ref[0]
        if aux_ref is not None:
            aux_ref[pl.ds(r0, rc), :] = y
        h_ref[pl.ds(r0, rc), :] = y.astype(BF16)
        return carry

    lax.fori_loop(0, tm // rc, body, 0)


def _nmm_kernel(*refs, modulated, with_aux):
    it = iter(refs)
    x_ref, g_ref = next(it), next(it)
    sh_ref = sc_ref = None
    if modulated:
        sh_ref, sc_ref = next(it), next(it)
    w_ref, o_ref = next(it), next(it)
    aux_ref = next(it) if with_aux else None
    h_ref = next(it)

    @pl.when(pl.program_id(1) == 0)
    def _():
        _norm_chunks(x_ref, h_ref, g_ref, sh_ref, sc_ref, aux_ref, 128)

    o_ref[...] = _dot(h_ref[...], w_ref[...])


def _nmm(x, col_block, k, g, w, wl, tn, rows=None, mods=None, layer=0, chunks=(0, 1), with_aux=False, tm=ROW_TILE):
    n_rows = x.shape[0]
    n = w.shape[2]
    in_specs = [pl.BlockSpec((tm, k), lambda i, j: (i, col_block)),
                pl.BlockSpec((1, k), lambda i, j: (0, 0))]
    args = [x, g.reshape(1, k)]
    if mods is not None:
        for c in chunks:
            in_specs.append(pl.BlockSpec(
                (1, 1, k), lambda i, j, c=c: (layer * MOD_ROWS + rows.mod_row(i), 0, c)))
            args.append(mods)
    in_specs.append(pl.BlockSpec((None, k, tn), lambda i, j: (wl, 0, j)))
    args.append(w)
    out_shape = [jax.ShapeDtypeStruct((n_rows, n), F32)]
    out_specs = [pl.BlockSpec((tm, tn), lambda i, j: (i, j))]
    if with_aux:
        out_shape.append(jax.ShapeDtypeStruct((n_rows, k), F32))
        out_specs.append(pl.BlockSpec((tm, k), lambda i, j: (i, 0)))
    res = pl.pallas_call(
        functools.partial(_nmm_kernel, modulated=mods is not None, with_aux=with_aux),
        out_shape=out_shape,
        grid=(n_rows // tm, n // tn),
        in_specs=in_specs,
        out_specs=out_specs,
        scratch_shapes=[pltpu.VMEM((tm, k), BF16)],
        compiler_params=_cparams(("parallel", "arbitrary")),
        name="norm_matmul",
    )(*args)
    return res if with_aux else res[0]


def _mm_kernel(a_ref, w_ref, o_ref):
    o_ref[...] = _dot(a_ref[...].astype(BF16), w_ref[...])


def _mm(a, w, wl, tm, tn):
    m, k = a.shape
    n = w.shape[2]
    return pl.pallas_call(
        _mm_kernel,
        out_shape=jax.ShapeDtypeStruct((m, n), F32),
        grid=(m // tm, n // tn),
        in_specs=[pl.BlockSpec((tm, k), lambda i, j: (i, 0)),
                  pl.BlockSpec((None, k, tn), lambda i, j: (wl, 0, j))],
        out_specs=pl.BlockSpec((tm, tn), lambda i, j: (i, j)),
        compiler_params=_cparams(("parallel", "parallel")),
        name="matmul",
    )(a, w)


def _proj_res_kernel(a1p_ref, a2p_ref, a1s_ref, a2s_ref, w1_ref, w2_ref, x_ref, gate_ref, o_ref, *, prompt_tiles):
    def emit(a1_ref, a2_ref):
        acc = _dot(a1_ref[...], w1_ref[...]) + _dot(a2_ref[...], w2_ref[...])
        o_ref[...] = x_ref[...] + gate_ref[0] * acc

    @pl.when(pl.program_id(0) < prompt_tiles)
    def _():
        emit(a1p_ref, a2p_ref)

    @pl.when(pl.program_id(0) >= prompt_tiles)
    def _():
        emit(a1s_ref, a2s_ref)


def _proj_res(a1p, a2p, a1s, a2s, w, wl, x, mods, rows, layer, chunk, tn=512):
    tm = rows.tm
    n_rows = x.shape[0]
    k1 = a1p.shape[1]
    n = w.shape[2]
    per = n // tn
    pt = rows.prompt_tiles

    def prow(i):
        return jnp.minimum(i, pt - 1)

    def srow(i):
        return jnp.maximum(i - pt, 0)

    return pl.pallas_call(
        functools.partial(_proj_res_kernel, prompt_tiles=pt),
        out_shape=jax.ShapeDtypeStruct((n_rows, n), F32),
        grid=(n_rows // tm, n // tn),
        in_specs=[
            pl.BlockSpec((tm, k1), lambda i, j: (prow(i), 0)),
            pl.BlockSpec((tm, k1), lambda i, j: (prow(i), 0)),
            pl.BlockSpec((tm, k1), lambda i, j: (srow(i), 0)),
            pl.BlockSpec((tm, k1), lambda i, j: (srow(i), 0)),
            pl.BlockSpec((None, k1, tn), lambda i, j: (wl, 0, j)),
            pl.BlockSpec((None, k1, tn), lambda i, j: (wl, 1, j)),
            pl.BlockSpec((tm, tn), lambda i, j: (i, j)),
            pl.BlockSpec((1, 1, tn), lambda i, j: (layer * MOD_ROWS + rows.mod_row(i), 0, chunk * per + j)),
        ],
        out_specs=pl.BlockSpec((tm, tn), lambda i, j: (i, j)),
        compiler_params=_cparams(("parallel", "parallel")),
        name="proj_residual",
    )(a1p, a2p, a1s, a2s, w, w, x, mods)


def _ffn_kernel(x_ref, g_ref, sh_ref, sc_ref, gate_ref, kp_ref, kn_ref,
                wg_ref, wu_ref, cwg_ref, cwu_ref, cbg_ref, cbu_ref, wd_ref, o_ref,
                h_ref, act_ref, *, nj):
    j = pl.program_id(1)
    tm = x_ref.shape[0]

    def conv(w_ref, cw_ref, cb_ref):
        u = _dot(h_ref[...], w_ref[...])
        prev = jnp.where(kp_ref[...] > 0.0, pltpu.roll(u, 1, axis=0), 0.0)
        nxt = jnp.where(kn_ref[...] > 0.0, pltpu.roll(u, tm - 1, axis=0), 0.0)
        return cb_ref[...] + prev * cw_ref[0:1, :] + u * cw_ref[1:2, :] + nxt * cw_ref[2:3, :]

    def stage(slot):
        cg = conv(wg_ref, cwg_ref, cbg_ref)
        cu = conv(wu_ref, cwu_ref, cbu_ref)
        act_ref[slot] = (cg * _sigmoid(cg) * cu).astype(BF16)

    def contract(slot):
        return _dot(act_ref[slot], wd_ref[...])

    @pl.when(j == 0)
    def _():
        _norm_chunks(x_ref, h_ref, g_ref, sh_ref, sc_ref, None, 128)
        o_ref[...] = x_ref[...]
        stage(0)

    @pl.when(jnp.logical_and(j > 0, j < nj))
    def _():
        slot = j % 2
        stage(slot)
        o_ref[...] += gate_ref[0] * contract(1 - slot)

    @pl.when(j == nj)
    def _():
        o_ref[...] += gate_ref[0] * contract((nj - 1) % 2)


def _ffn(x, g, mods, rows, layer, w_up, conv_w, conv_b, w_down, keep_prev, keep_next, tn=512):
    n_rows, d = x.shape
    tm = rows.tm
    d_ff = w_down.shape[1]
    nj = d_ff // tn

    def mod_spec(c):
        return pl.BlockSpec((1, 1, d), lambda i, j: (layer * MOD_ROWS + rows.mod_row(i), 0, c))

    def up(j):
        return jnp.minimum(j, nj - 1)

    def down(j):
        return jnp.maximum(j - 1, 0)

    once = pl.Buffered(1)
    return pl.pallas_call(
        functools.partial(_ffn_kernel, nj=nj),
        out_shape=jax.ShapeDtypeStruct((n_rows, d), F32),
        grid=(n_rows // tm, nj + 1),
        in_specs=[
            pl.BlockSpec((tm, d), lambda i, j: (i, 0)),
            pl.BlockSpec((1, d), lambda i, j: (0, 0)),
            mod_spec(3), mod_spec(4), mod_spec(5),
            pl.BlockSpec((tm, 1), lambda i, j: (i, 0)),
            pl.BlockSpec((tm, 1), lambda i, j: (i, 0)),
            pl.BlockSpec((None, d, tn), lambda i, j: (layer, 0, up(j))),
            pl.BlockSpec((None, d, tn), lambda i, j: (layer, 0, nj + up(j))),
            pl.BlockSpec((3, tn), lambda i, j: (0, up(j))),
            pl.BlockSpec((3, tn), lambda i, j: (0, nj + up(j))),
            pl.BlockSpec((1, tn), lambda i, j: (0, up(j))),
            pl.BlockSpec((1, tn), lambda i, j: (0, nj + up(j))),
            pl.BlockSpec((None, tn, d), lambda i, j: (layer, down(j), 0)),
        ],
        out_specs=pl.BlockSpec((tm, d), lambda i, j: (i, 0), pipeline_mode=once),
        scratch_shapes=[pltpu.VMEM((tm, d), BF16), pltpu.VMEM((2, tm, tn), BF16)],
        compiler_params=_cparams(("parallel", "arbitrary")),
        name="conv_ffn",
    )(x, g.reshape(1, d), mods, mods, mods, keep_prev, keep_next,
      w_up, w_up, conv_w, conv_w, conv_b.reshape(1, -1), conv_b.reshape(1, -1), w_down)


def _final_norm_kernel(x_ref, g_ref, o_ref):
    o_ref[...] = _rms_rows(x_ref[...], g_ref[...])


def _final_norm(x, g, row0, n_rows, tm=256):
    d = x.shape[1]
    return pl.pallas_call(
        _final_norm_kernel,
        out_shape=jax.ShapeDtypeStruct((n_rows, d), F32),
        grid=(n_rows // tm,),
        in_specs=[pl.BlockSpec((tm, d), lambda i: (row0 // tm + i, 0)), pl.BlockSpec((1, d), lambda i: (0, 0))],
        out_specs=pl.BlockSpec((tm, d), lambda i: (i, 0)),
        compiler_params=_cparams(("parallel",)),
        name="final_norm",
    )(x, g.reshape(1, d))


def _even_attn_kernel(*refs, past, use_rope, emit_cache, n_alias, lam_init):
    it = iter(refs)
    qa_ref, qb_ref = next(it), next(it)
    ka_ref, va_ref, kb_ref, vb_ref = next(it), next(it), next(it), next(it)
    lam_ref, subln_ref, qn_ref, kn_ref = next(it), next(it), next(it), next(it)
    if past:
        cdak_ref, cdav_ref, cgqk_ref, cgqv_ref = next(it), next(it), next(it), next(it)
    if use_rope:
        q64_ref, k64_ref, q128_ref, k128_ref = next(it), next(it), next(it), next(it)
    for _ in range(n_alias):
        next(it)
    oa_ref, ob_ref = next(it), next(it)
    if emit_cache:
        dak_ref, dav_ref, gqk_ref, gqv_ref = next(it), next(it), next(it), next(it)
    kda_s, vda_s, kgq_s, vgq_s = next(it), next(it), next(it), next(it)

    t_len = ka_ref.shape[0]

    @pl.when(pl.program_id(1) == 0)
    def _():
        for h in range(DA_HEADS):
            cols = slice(h * HEAD_DIM, (h + 1) * HEAD_DIM)
            k = ka_ref[:, cols]
            v = va_ref[:, cols]
            if emit_cache:
                dak_ref[h, 0] = k[:, :DA_SUB]
                dak_ref[h, 1] = k[:, DA_SUB:]
                dav_ref[h] = v
            if use_rope:
                k = _rope_lanes(k, k64_ref[0], k64_ref[1], k64_ref[2], DA_SUB // 4)
            kda_s[h, 0, past:, :] = k[:, :DA_SUB].astype(BF16)
            kda_s[h, 1, past:, :] = k[:, DA_SUB:].astype(BF16)
            vda_s[h, past:, :] = v.astype(BF16)
            if past:
                kda_s[h, 0, :past, :] = cdak_ref[h, 0].astype(BF16)
                kda_s[h, 1, :past, :] = cdak_ref[h, 1].astype(BF16)
                vda_s[h, :past, :] = cdav_ref[h].astype(BF16)
        for g in range(GQ_KV_HEADS):
            cols = slice(g * HEAD_DIM, (g + 1) * HEAD_DIM)
            k = _rms_rows(kb_ref[:, cols], kn_ref[...])
            v = vb_ref[:, cols]
            if emit_cache:
                gqk_ref[g] = k
                gqv_ref[g] = v
            if use_rope:
                k = _rope_lanes(k, k128_ref[0], k128_ref[1], k128_ref[2], HEAD_DIM // 4)
            kgq_s[g, past:, :] = k.astype(BF16)
            vgq_s[g, past:, :] = v.astype(BF16)
            if past:
                kgq_s[g, :past, :] = cgqk_ref[g].astype(BF16)
                vgq_s[g, :past, :] = cgqv_ref[g].astype(BF16)

    lam4 = lam_ref[...]
    lam = (jnp.exp(jnp.sum(lam4[0:1] * lam4[1:2], axis=-1, keepdims=True))
           - jnp.exp(jnp.sum(lam4[2:3] * lam4[3:4], axis=-1, keepdims=True)) + lam_init)

    da_scale = DA_SUB ** -0.5
    for h in range(DA_HEADS):
        cols = slice(h * HEAD_DIM, (h + 1) * HEAD_DIM)
        q = qa_ref[:, cols]
        if use_rope:
            q = _rope_lanes(q, q64_ref[0], q64_ref[1], q64_ref[2], DA_SUB // 4)
        a1 = _softmax_rows(_dot_nt(q[:, :DA_SUB].astype(BF16), kda_s[h, 0]) * da_scale)
        a2 = _softmax_rows(_dot_nt(q[:, DA_SUB:].astype(BF16), kda_s[h, 1]) * da_scale)
        o = _dot((a1 - lam * a2).astype(BF16), vda_s[h])
        oa_ref[:, cols] = (_rms_rows(o, subln_ref[...]) * (1.0 - lam_init)).astype(BF16)

    gq_scale = HEAD_DIM ** -0.5
    for h in range(GQ_HEADS):
        cols = slice(h * HEAD_DIM, (h + 1) * HEAD_DIM)
        q = _rms_rows(qb_ref[:, cols], qn_ref[...])
        if use_rope:
            q = _rope_lanes(q, q128_ref[0], q128_ref[1], q128_ref[2], HEAD_DIM // 4)
        a = _softmax_rows(_dot_nt(q.astype(BF16), kgq_s[h // GQ_REP]) * gq_scale)
        ob_ref[:, cols] = _dot(a.astype(BF16), vgq_s[h // GQ_REP]).astype(BF16)


def _even_attn(proj, row0, nb, t_len, tq, lam, subln_g, qn_g, kn_g, lam_init,
               caches=None, ropes=None, emit=None):
    n_rows = proj.shape[0]
    hd = DA_HEADS * HEAD_DIM
    kvw = GQ_KV_HEADS * HEAD_DIM
    nq = t_len // tq
    qb0 = row0 // tq
    kb0 = row0 // t_len
    past = caches[0].shape[3] if caches is not None else 0
    s_len = past + t_len

    def qspec(cb):
        return pl.BlockSpec((tq, hd), lambda b, q: (qb0 + b * nq + q, cb))

    def kspec(width, cb):
        return pl.BlockSpec((t_len, width), lambda b, q: (kb0 + b, cb))

    def const(shape):
        return pl.BlockSpec(shape, lambda b, q: (0,) * len(shape))

    in_specs = [qspec(0), qspec(3), kspec(hd, 1), kspec(hd, 2), kspec(kvw, 4 * hd // kvw), kspec(kvw, 4 * hd // kvw + 1),
                const((4, DA_SUB)), const((1, HEAD_DIM)), const((1, HEAD_DIM)), const((1, HEAD_DIM))]
    args = [proj, proj, proj, proj, proj, proj, lam, subln_g.reshape(1, -1), qn_g.reshape(1, -1), kn_g.reshape(1, -1)]
    if caches is not None:
        cdak, cdav, cgqk, cgqv = caches
        in_specs += [
            pl.BlockSpec((None, DA_HEADS, 2, past, DA_SUB), lambda b, q: (b, 0, 0, 0, 0)),
            pl.BlockSpec((None, DA_HEADS, past, HEAD_DIM), lambda b, q: (b, 0, 0, 0)),
            pl.BlockSpec((None, GQ_KV_HEADS, past, HEAD_DIM), lambda b, q: (b, 0, 0, 0)),
            pl.BlockSpec((None, GQ_KV_HEADS, past, HEAD_DIM), lambda b, q: (b, 0, 0, 0)),
        ]
        args += [cdak, cdav, cgqk, cgqv]
    if ropes is not None:
        r64, r128 = ropes
        in_specs += [pl.BlockSpec((3, tq, LANES), lambda b, q: (0, q, 0)),
                     pl.BlockSpec((3, t_len, LANES), lambda b, q: (0, 0, 0)),
                     pl.BlockSpec((3, tq, LANES), lambda b, q: (0, q, 0)),
                     pl.BlockSpec((3, t_len, LANES), lambda b, q: (0, 0, 0))]
        args += [r64, r64, r128, r128]
    out_shape = [jax.ShapeDtypeStruct((nb * t_len, hd), BF16), jax.ShapeDtypeStruct((nb * t_len, hd), BF16)]
    out_specs = [pl.BlockSpec((tq, hd), lambda b, q: (b * nq + q, 0)),
                 pl.BlockSpec((tq, hd), lambda b, q: (b * nq + q, 0))]
    aliases = {}
    if emit is not None:
        slot, n_slots, prev = emit
        out_shape += [jax.ShapeDtypeStruct((nb, n_slots, DA_HEADS, 2, t_len, DA_SUB), F32),
                      jax.ShapeDtypeStruct((nb, n_slots, DA_HEADS, t_len, HEAD_DIM), F32),
                      jax.ShapeDtypeStruct((nb, n_slots, GQ_KV_HEADS, t_len, HEAD_DIM), F32),
                      jax.ShapeDtypeStruct((nb, n_slots, GQ_KV_HEADS, t_len, HEAD_DIM), F32)]
        out_specs += [pl.BlockSpec((None, None, DA_HEADS, 2, t_len, DA_SUB), lambda b, q: (b, slot, 0, 0, 0, 0)),
                      pl.BlockSpec((None, None, DA_HEADS, t_len, HEAD_DIM), lambda b, q: (b, slot, 0, 0, 0)),
                      pl.BlockSpec((None, None, GQ_KV_HEADS, t_len, HEAD_DIM), lambda b, q: (b, slot, 0, 0, 0)),
                      pl.BlockSpec((None, None, GQ_KV_HEADS, t_len, HEAD_DIM), lambda b, q: (b, slot, 0, 0, 0))]
        if prev is not None:
            aliases = {len(args) + i: 2 + i for i in range(len(prev))}
            in_specs += [pl.BlockSpec(memory_space=pl.ANY)] * len(prev)
            args += list(prev)
    return pl.pallas_call(
        functools.partial(_even_attn_kernel, past=past, use_rope=ropes is not None,
                          emit_cache=emit is not None, n_alias=len(aliases), lam_init=lam_init),
        input_output_aliases=aliases,
        out_shape=out_shape,
        grid=(nb, nq),
        in_specs=in_specs,
        out_specs=out_specs,
        scratch_shapes=[pltpu.VMEM((DA_HEADS, 2, s_len, DA_SUB), BF16),
                        pltpu.VMEM((DA_HEADS, s_len, HEAD_DIM), BF16),
                        pltpu.VMEM((GQ_KV_HEADS, s_len, HEAD_DIM), BF16),
                        pltpu.VMEM((GQ_KV_HEADS, s_len, HEAD_DIM), BF16)],
        compiler_params=_cparams(("parallel", "arbitrary")),
        name="even_attention",
    )(*args)


def _mlstm_chunk(qc, kc, vc, lf_r, lf_c, li_r, li_c, state, reverse):
    n = qc.shape[0]
    ti = lax.broadcasted_iota(jnp.int32, (n, n), 0)
    si = lax.broadcasted_iota(jnp.int32, (n, n), 1)
    seen = (si >= ti) if reverse else (si <= ti)
    b_c = jnp.sum(jnp.where(seen, lf_r, 0.0), axis=1, keepdims=True)
    seen_t = (ti >= si) if reverse else (ti <= si)
    b_r = jnp.sum(jnp.where(seen_t, lf_c, 0.0), axis=0, keepdims=True)
    b_end = b_r[:, 0:1] if reverse else b_r[:, n - 1:n]
    d_intra = jnp.where(seen, b_c - b_r + li_r, -jnp.inf)
    row_max = jnp.max(d_intra, axis=1, keepdims=True)
    qb, kb, vb = qc.astype(BF16), kc.astype(BF16), vc.astype(BF16)
    if state is None:
        m_prev = jnp.zeros((1, 1), F32)
    else:
        c_prev, n_prev, m_prev = state
    d_inter = b_c + m_prev
    m_t = jnp.maximum(d_inter, row_max)
    s = _dot_nt(qb, kb) * jnp.exp(d_intra - m_t)
    num = _dot(s.astype(BF16), vb)
    den = jnp.sum(s, axis=1, keepdims=True)
    if state is not None:
        a = jnp.exp(d_inter - m_t)
        num = num + a * _dot(qb, c_prev.astype(BF16))
        den = den + a * jnp.sum(qc * n_prev, axis=1, keepdims=True)
    h = num / jnp.maximum(jnp.abs(den), jnp.exp(-m_t))
    g_c = b_end - b_c + li_c
    g_r = b_end - b_r + li_r
    m_new = jnp.maximum(b_end + m_prev, jnp.max(g_r, axis=1, keepdims=True))
    kw = kc * jnp.exp(g_c - m_new)
    c_new = _dot(kw.T.astype(BF16), vb)
    n_new = jnp.sum(kw, axis=0, keepdims=True)
    if state is not None:
        decay = jnp.exp(b_end + m_prev - m_new)
        c_new = decay * c_prev + c_new
        n_new = decay * n_prev + n_new
    return h, (c_new, n_new, m_new)


def _mlstm_kernel(*refs, has_state, emit_state, n_alias, chunk, hps):
    it = iter(refs)
    q_ref, k_ref, v_ref, o_ref, gt_ref = next(it), next(it), next(it), next(it), next(it)
    cwq_ref, cwk_ref, cbq_ref, cbk_ref, gb_ref, ng_ref = next(it), next(it), next(it), next(it), next(it), next(it)
    if has_state:
        c0_ref, n0_ref, m0_ref = next(it), next(it), next(it)
    for _ in range(n_alias):
        next(it)
    mix_ref = next(it)
    if emit_state:
        co_ref, no_ref, mo_ref = next(it), next(it), next(it)
    hacc = next(it)

    t_len = q_ref.shape[0]
    row = lax.broadcasted_iota(jnp.int32, (t_len, 1), 0)

    def conv_silu(x, w, b):
        prev = jnp.where(row == 0, 0.0, pltpu.roll(x, 1, axis=0))
        nxt = jnp.where(row == t_len - 1, 0.0, pltpu.roll(x, t_len - 1, axis=0))
        y = b + prev * w[0:1, :] + x * w[1:2, :] + nxt * w[2:3, :]
        return y * _sigmoid(y)

    gates = gt_ref[...] + gb_ref[...]
    gates_ls = jnp.minimum(gates, 0.0) - jnp.log(1.0 + jnp.exp(-jnp.abs(gates)))
    gates_t, gates_ls_t = gates.T, gates_ls.T
    lane = lax.broadcasted_iota(jnp.int32, gates.shape, 1)
    sub = lax.broadcasted_iota(jnp.int32, gates_t.shape, 0)

    def gate_col(src, kind, head):
        idx = MLA_ROPE + kind * ML_HEADS + head
        if isinstance(head, int):
            return src[:, idx:idx + 1]
        return jnp.sum(jnp.where(lane == idx, src, 0.0), axis=1, keepdims=True)

    def gate_row(src_t, kind, head):
        idx = MLA_ROPE + kind * ML_HEADS + head
        if isinstance(head, int):
            return src_t[idx:idx + 1, :]
        return jnp.sum(jnp.where(sub == idx, src_t, 0.0), axis=0, keepdims=True)

    n_chunks = t_len // chunk
    for hh in range(hps):
        head = hh if hps == ML_HEADS else pl.program_id(1) * hps + hh
        cols = slice(hh * ML_DK, (hh + 1) * ML_DK)
        q = conv_silu(q_ref[:, cols], cwq_ref[:, cols], cbq_ref[:, cols]) * (ML_DK ** -0.5)
        k = conv_silu(k_ref[:, cols], cwk_ref[:, cols], cbk_ref[:, cols])
        v = v_ref[:, cols]
        for d in range(2):
            li_c, li_r = gate_col(gates, 2 * d, head), gate_row(gates_t, 2 * d, head)
            lf_c, lf_r = gate_col(gates_ls, 2 * d + 1, head), gate_row(gates_ls_t, 2 * d + 1, head)
            state = (c0_ref[d, hh], n0_ref[d, hh], m0_ref[d, hh][:, 0:1]) if has_state else None
            order = range(n_chunks) if d == 0 else range(n_chunks - 1, -1, -1)
            for c in order:
                sl = slice(c * chunk, (c + 1) * chunk)
                h, state = _mlstm_chunk(q[sl], k[sl], v[sl], lf_r[:, sl], lf_c[sl], li_r[:, sl], li_c[sl],
                                        state, reverse=(d == 1))
                if d == 0:
                    hacc[sl, cols] = h
                else:
                    hacc[sl, cols] += h
            if emit_state:
                co_ref[d, hh] = state[0]
                no_ref[d, hh] = state[1]
                mo_ref[d, hh] = jnp.broadcast_to(state[2], (1, LANES))
        hn = _rms_rows(hacc[:, cols], ng_ref[...])
        mix_ref[:, cols] = (hn * _sigmoid(o_ref[:, cols])).astype(BF16)


def _mlstm(proj, row0, nb, t_len, conv_w, conv_b, gate_b_lanes, norm_g, hps, states=None, emit=None):
    hw = ML_HEADS * ML_DK
    kb0 = row0 // t_len
    ng = ML_HEADS // hps
    wid = hps * ML_DK
    chunk = min(ML_CHUNK, t_len)
    assert t_len % chunk == 0
    gate_block = (4 * hw + MLA_Q_RANK + MLA_KV_RANK) // LANES

    def tok(cb):
        return pl.BlockSpec((t_len, wid), lambda b, h: (kb0 + b, cb * ng + h))

    def state_spec(minor):
        return pl.BlockSpec((None, 2, hps) + minor, lambda b, h: (b, 0, h, 0, 0))

    in_specs = [tok(0), tok(1), tok(2), tok(3),
                pl.BlockSpec((t_len, LANES), lambda b, h: (kb0 + b, gate_block)),
                pl.BlockSpec((3, wid), lambda b, h: (0, h)),
                pl.BlockSpec((3, wid), lambda b, h: (0, ng + h)),
                pl.BlockSpec((1, wid), lambda b, h: (0, h)),
                pl.BlockSpec((1, wid), lambda b, h: (0, ng + h)),
                pl.BlockSpec((1, LANES), lambda b, h: (0, 0)),
                pl.BlockSpec((1, ML_DV), lambda b, h: (0, 0))]
    args = [proj, proj, proj, proj, proj, conv_w, conv_w, conv_b.reshape(1, -1), conv_b.reshape(1, -1),
            gate_b_lanes, norm_g.reshape(1, -1)]
    if states is not None:
        in_specs += [state_spec((ML_DK, ML_DV)), state_spec((1, ML_DK)), state_spec((1, LANES))]
        args += list(states)
    out_shape = [jax.ShapeDtypeStruct((nb * t_len, hw), BF16)]
    out_specs = [pl.BlockSpec((t_len, wid), lambda b, h: (b, h))]
    aliases = {}
    if emit is not None:
        slot, n_slots, prev = emit

        def out_state_spec(minor):
            return pl.BlockSpec((None, None, 2, hps) + minor, lambda b, h: (b, slot, 0, h, 0, 0))

        out_shape += [jax.ShapeDtypeStruct((nb, n_slots, 2, ML_HEADS, ML_DK, ML_DV), F32),
                      jax.ShapeDtypeStruct((nb, n_slots, 2, ML_HEADS, 1, ML_DK), F32),
                      jax.ShapeDtypeStruct((nb, n_slots, 2, ML_HEADS, 1, LANES), F32)]
        out_specs += [out_state_spec((ML_DK, ML_DV)), out_state_spec((1, ML_DK)), out_state_spec((1, LANES))]
        if prev is not None:
            aliases = {len(args) + i: 1 + i for i in range(len(prev))}
            in_specs += [pl.BlockSpec(memory_space=pl.ANY)] * len(prev)
            args += list(prev)
    return pl.pallas_call(
        functools.partial(_mlstm_kernel, has_state=states is not None, emit_state=emit is not None,
                          n_alias=len(aliases), chunk=chunk, hps=hps),
        input_output_aliases=aliases,
        out_shape=out_shape,
        grid=(nb, ng),
        in_specs=in_specs,
        out_specs=out_specs,
        scratch_shapes=[pltpu.VMEM((t_len, wid), F32)],
        compiler_params=_cparams(("parallel", "parallel")),
        name="mlstm",
    )(*args)


def _mla_attn_kernel(*refs, past, use_rope):
    it = iter(refs)
    cq_ref, kv_ref, kpe_ref = next(it), next(it), next(it)
    if past:
        kvc_ref, kpec_ref = next(it), next(it)
    if use_rope:
        q64_ref, k64_ref = next(it), next(it)
    o_ref = next(it)
    kn_s, v_s, kpe_s = next(it), next(it), next(it)

    nope_w = MLA_HEADS * MLA_NOPE

    @pl.when(pl.program_id(1) == 0)
    def _():
        for h in range(MLA_HEADS):
            c0 = h * (MLA_NOPE + MLA_V)
            kn_s[h, past:, :] = kv_ref[:, c0:c0 + MLA_NOPE].astype(BF16)
            v_s[h, past:, :] = kv_ref[:, c0 + MLA_NOPE:c0 + MLA_NOPE + MLA_V].astype(BF16)
            if past:
                kn_s[h, :past, :] = kvc_ref[:, c0:c0 + MLA_NOPE].astype(BF16)
                v_s[h, :past, :] = kvc_ref[:, c0 + MLA_NOPE:c0 + MLA_NOPE + MLA_V].astype(BF16)
        kpe = kpe_ref[...]
        if use_rope:
            kpe = _rope_lanes(kpe, k64_ref[0], k64_ref[1], k64_ref[2], MLA_ROPE // 4)
        kpe_s[past:, :] = kpe[:, :MLA_ROPE].astype(BF16)
        if past:
            kpe_s[:past, :] = kpec_ref[...].astype(BF16)

    scale = (MLA_NOPE + MLA_ROPE) ** -0.5
    for h2 in range(MLA_HEADS // 2):
        qp2 = cq_ref[:, nope_w + h2 * LANES:nope_w + (h2 + 1) * LANES]
        if use_rope:
            qp2 = _rope_lanes(qp2, q64_ref[0], q64_ref[1], q64_ref[2], MLA_ROPE // 4)
        for e in range(2):
            h = 2 * h2 + e
            qn = cq_ref[:, h * MLA_NOPE:(h + 1) * MLA_NOPE].astype(BF16)
            qp = qp2[:, e * MLA_ROPE:(e + 1) * MLA_ROPE].astype(BF16)
            s = (_dot_nt(qn, kn_s[h]) + _dot_nt(qp, kpe_s[...])) * scale
            a = _softmax_rows(s)
            o_ref[:, h * MLA_V:(h + 1) * MLA_V] = _dot(a.astype(BF16), v_s[h]).astype(BF16)


def _mla_attn(cq, kv, proj, row0, nb, t_len, tq, cache=None, rope64=None):
    n_rows = cq.shape[0]
    nq = t_len // tq
    qb0 = row0 // tq
    kb0 = row0 // t_len
    past = cache[0].shape[0] // nb if cache is not None else 0
    s_len = past + t_len
    kpe_block = (4 * ML_HEADS * ML_DK + MLA_Q_RANK + MLA_KV_RANK) // LANES
    in_specs = [pl.BlockSpec((tq, cq.shape[1]), lambda b, q: (qb0 + b * nq + q, 0)),
                pl.BlockSpec((t_len, kv.shape[1]), lambda b, q: (kb0 + b, 0)),
                pl.BlockSpec((t_len, LANES), lambda b, q: (kb0 + b, kpe_block))]
    args = [cq, kv, proj]
    if cache is not None:
        kvc, kpec = cache
        in_specs += [pl.BlockSpec((past, kvc.shape[1]), lambda b, q: (b, 0)),
                     pl.BlockSpec((None, past, MLA_ROPE), lambda b, q: (b, 0, 0))]
        args += [kvc, kpec]
    if rope64 is not None:
        in_specs += [pl.BlockSpec((3, tq, LANES), lambda b, q: (0, q, 0)),
                     pl.BlockSpec((3, t_len, LANES), lambda b, q: (0, 0, 0))]
        args += [rope64, rope64]
    return pl.pallas_call(
        functools.partial(_mla_attn_kernel, past=past, use_rope=rope64 is not None),
        out_shape=jax.ShapeDtypeStruct((nb * t_len, MLA_HEADS * MLA_V), BF16),
        grid=(nb, nq),
        in_specs=in_specs,
        out_specs=pl.BlockSpec((tq, MLA_HEADS * MLA_V), lambda b, q: (b * nq + q, 0)),
        scratch_shapes=[pltpu.VMEM((MLA_HEADS, s_len, MLA_NOPE), BF16),
                        pltpu.VMEM((MLA_HEADS, s_len, MLA_V), BF16),
                        pltpu.VMEM((s_len, MLA_ROPE), BF16)],
        compiler_params=_cparams(("parallel", "arbitrary")),
        name="mla_attention",
    )(*args)


def kernel(x_prompt, x_sample, cache_da_k, cache_da_v, cache_gq_k, cache_gq_v, cache_mla_ckv, cache_mla_kpe, state_ml_C, state_ml_n, state_ml_m, c, c_ctx, norm1_g, norm2_g, w_mod, b_mod, even_w_in, even_w_out, da_lambda, da_subln_g, gq_q_norm_g, gq_k_norm_g, odd_w_in, odd_w_out, ml_conv_w, ml_conv_b, ml_gate_b, ml_norm_g, mla_q_norm_g, mla_w_uq, mla_kv_norm_g, mla_w_ukv, ffn_w_up, ffn_conv_w, ffn_conv_b, ffn_w_down, final_norm_g):
    bp, tp, d = x_prompt.shape
    bs, ts, _ = x_sample.shape
    depth = norm1_g.shape[0]
    past = cache_da_k.shape[4]
    n_prompt = bp * tp
    rows = _Rows(bp, tp, bs, ts, ROW_TILE)
    n_cond = 1 + bs
    assert n_cond <= MOD_ROWS

    x = jnp.concatenate([x_prompt.reshape(n_prompt, d), x_sample.reshape(bs * ts, d)], axis=0)

    cond_t = jnp.zeros((d, LANES), F32).at[:, :n_cond].set(jnp.concatenate([c_ctx[None, :], c], axis=0).T)
    mods = _adaln(cond_t, w_mod, b_mod, n_cond).reshape(depth * MOD_ROWS, 1, 6 * d)

    pos = jnp.concatenate([jnp.tile(jnp.arange(tp), bp), jnp.tile(jnp.arange(ts), bs)])
    seq_len = jnp.concatenate([jnp.full((n_prompt,), tp), jnp.full((bs * ts,), ts)])
    keep_prev = (pos != 0).astype(F32)[:, None]
    keep_next = (pos != seq_len - 1).astype(F32)[:, None]

    rope64 = jnp.stack(_rope_tables(ts, DA_SUB))
    rope128 = jnp.stack(_rope_tables(ts, HEAD_DIM))

    hw = ML_HEADS * ML_DK
    o_gate = 4 * hw
    o_qd = o_gate + 4 * ML_HEADS
    o_kvd = o_qd + MLA_Q_RANK
    o_kpe = o_kvd + MLA_KV_RANK
    n_in = odd_w_in.shape[2]
    n_pad = -(-n_in // 512) * 512
    odd_w_in_b = jnp.concatenate(
        [odd_w_in[:, :, :o_gate], odd_w_in[:, :, o_qd:o_kvd], odd_w_in[:, :, o_kvd:o_kpe], odd_w_in[:, :, o_kpe:],
         odd_w_in[:, :, o_gate:o_qd], jnp.zeros(odd_w_in.shape[:2] + (n_pad - n_in,), F32)], axis=2).astype(BF16)
    w_uq = mla_w_uq.reshape(-1, MLA_Q_RANK, MLA_HEADS, MLA_NOPE + MLA_ROPE)
    w_uq_b = jnp.concatenate([w_uq[..., :MLA_NOPE].reshape(-1, MLA_Q_RANK, MLA_HEADS * MLA_NOPE),
                              w_uq[..., MLA_NOPE:].reshape(-1, MLA_Q_RANK, MLA_HEADS * MLA_ROPE)], axis=2).astype(BF16)
    even_w_in_b, even_w_out_b = even_w_in.astype(BF16), even_w_out.astype(BF16)
    odd_w_out_b, w_ukv_b = odd_w_out.astype(BF16), mla_w_ukv.astype(BF16)
    ffn_w_up_b, ffn_w_down_b = ffn_w_up.astype(BF16), ffn_w_down.astype(BF16)

    tq_s = min(256, ts)
    n_even, n_odd = (depth + 1) // 2, depth // 2
    even_caches = [jnp.zeros((bp, n_even, DA_HEADS, 2, tp, DA_SUB), F32),
                   jnp.zeros((bp, n_even, DA_HEADS, tp, HEAD_DIM), F32),
                   jnp.zeros((bp, n_even, GQ_KV_HEADS, tp, HEAD_DIM), F32),
                   jnp.zeros((bp, n_even, GQ_KV_HEADS, tp, HEAD_DIM), F32)]
    ml_states = [jnp.zeros((bp, n_odd, 2, ML_HEADS, ML_DK, ML_DV), F32),
                 jnp.zeros((bp, n_odd, 2, ML_HEADS, 1, ML_DK), F32),
                 jnp.zeros((bp, n_odd, 2, ML_HEADS, 1, LANES), F32)]
    new_ckv, new_kpe = [], []

    for l in range(depth):
        j = l // 2
        if l % 2 == 0:
            lam_init = 0.8 - 0.6 * math.exp(-0.3 * l)
            proj = _nmm(x, 0, d, norm1_g[l], even_w_in_b, j, 512, rows=rows, mods=mods, layer=l)
            common = (da_lambda[j], da_subln_g[j], gq_q_norm_g[j], gq_k_norm_g[j], lam_init)
            a1p, a2p, *even_caches = _even_attn(proj, 0, bp, tp, tp, *common, emit=(j, n_even, even_caches))
            a1s, a2s = _even_attn(proj, n_prompt, bs, ts, tq_s, *common,
                                  caches=(cache_da_k[:, j], cache_da_v[:, j], cache_gq_k[:, j], cache_gq_v[:, j]),
                                  ropes=(rope64, rope128))
            w_out = even_w_out_b
        else:
            proj = _nmm(x, 0, d, norm1_g[l], odd_w_in_b, j, 512, rows=rows, mods=mods, layer=l)
            gate_b_lanes = jnp.zeros((1, LANES), F32).at[0, MLA_ROPE:MLA_ROPE + 4 * ML_HEADS].set(ml_gate_b[j])

            a1p, *ml_states = _mlstm(proj, 0, bp, tp, ml_conv_w[j], ml_conv_b[j], gate_b_lanes,
                                     ml_norm_g[j], ML_HEADS, emit=(j, n_odd, ml_states))
            m0 = jnp.broadcast_to(state_ml_m[:, j][..., None, None], (bs, 2, ML_HEADS, 1, LANES))
            (a1s,) = _mlstm(proj, n_prompt, bs, ts, ml_conv_w[j], ml_conv_b[j], gate_b_lanes, ml_norm_g[j], 1,
                            states=(state_ml_C[:, j], state_ml_n[:, j][:, :, :, None, :], m0))

            cq = _nmm(proj, (4 * hw) // MLA_Q_RANK, MLA_Q_RANK, mla_q_norm_g[j], w_uq_b, j, 512)
            kv, ckv = _nmm(proj, (4 * hw + MLA_Q_RANK) // MLA_KV_RANK, MLA_KV_RANK, mla_kv_norm_g[j], w_ukv_b, j, 512,
                           with_aux=True)
            kv_cache = _mm(cache_mla_ckv[:, j].reshape(bs * past, MLA_KV_RANK), w_ukv_b, j, past, 512)
            new_ckv.append(ckv[:n_prompt].reshape(bp, tp, MLA_KV_RANK))
            kpe_col = 4 * hw + MLA_Q_RANK + MLA_KV_RANK
            new_kpe.append(proj[:n_prompt, kpe_col:kpe_col + MLA_ROPE].reshape(bp, tp, MLA_ROPE))

            a2p = _mla_attn(cq, kv, proj, 0, bp, tp, tp)
            a2s = _mla_attn(cq, kv, proj, n_prompt, bs, ts, tq_s,
                            cache=(kv_cache, cache_mla_kpe[:, j]), rope64=rope64)
            w_out = odd_w_out_b

        x = _proj_res(a1p, a2p, a1s, a2s, w_out, j, x, mods, rows, l, 2)
        x = _ffn(x, norm2_g[l], mods, rows, l, ffn_w_up_b, ffn_conv_w[l], ffn_conv_b[l],
                 ffn_w_down_b, keep_prev, keep_next)

    y_prompt = _final_norm(x, final_norm_g, 0, n_prompt).reshape(bp, tp, d)
    y_sample = _final_norm(x, final_norm_g, n_prompt, bs * ts).reshape(bs, ts, d)
    new_c, new_n, new_m = ml_states
    return (y_prompt, y_sample, *even_caches,
            jnp.stack(new_ckv, axis=1), jnp.stack(new_kpe, axis=1),
            new_c, new_n[:, :, :, :, 0, :], new_m[:, :, :, :, 0, 0])
```
